```python
import math
import jax
import jax.numpy as jnp
from jax import lax
import numpy as np

D_MODEL = 1024
BATCH = 16
SEQ = 256
DEPTH = 2
DEC_BATCH = 8
DEC_SEQ = 2048
PAST_LEN = 256

GRID_W = 64
Q_BLOCK = 128
ROPE_THETA = 10000.0
RMS_EPS = 1e-6
LN_EPS = 1e-5

CONV_CH = D_MODEL // 2
CONV_WIDTH = 31
CONV_PAD = CONV_WIDTH // 2
HEAD_DIM = 64
GQA_HEADS = (D_MODEL // 2) // HEAD_DIM
GQA_KV_HEADS = 2
GQA_GROUP = GQA_HEADS // GQA_KV_HEADS
EVEN_SPLITS = [2 * CONV_CH, 2 * CONV_CH + GQA_HEADS * HEAD_DIM, 2 * CONV_CH + (GQA_HEADS + GQA_KV_HEADS) * HEAD_DIM]
EVEN_IN = 2 * CONV_CH + (GQA_HEADS + 2 * GQA_KV_HEADS) * HEAD_DIM
EVEN_OUT = CONV_CH + GQA_HEADS * HEAD_DIM

DIFF_HD = 64
DIFF_HEADS = (D_MODEL // 2) // (2 * DIFF_HD)
DIFF_W = DIFF_HEADS * 2 * DIFF_HD
LAMBDA_INIT = 0.8 - 0.6 * math.exp(-0.3 * 1)
MLA_HEADS = 8
MLA_NOPE = 64
MLA_ROPE = 32
MLA_QK = MLA_NOPE + MLA_ROPE
MLA_V = 64
MLA_Q_RANK = 256
MLA_KV_RANK = 128
ODD_SPLITS = [DIFF_W, 2 * DIFF_W, 3 * DIFF_W, 3 * DIFF_W + MLA_Q_RANK, 3 * DIFF_W + MLA_Q_RANK + MLA_KV_RANK]
ODD_IN = 3 * DIFF_W + MLA_Q_RANK + MLA_KV_RANK + MLA_ROPE
ODD_OUT = DIFF_W + MLA_HEADS * MLA_V

D_FF = 2816
N_EXPERTS = 8
MOE_TOP_K = 2
D_FF_EXPERT = 1408

kernel_name = "hybrid_diffusion_prefix_trunk_step"


def rms_norm(x, g):
    xf = x.astype(jnp.float32)
    y = xf * lax.rsqrt(jnp.mean(xf * xf, axis=-1, keepdims=True) + RMS_EPS)
    return (y * g.astype(jnp.float32)).astype(x.dtype)


def layer_norm(x, g, b):
    xf = x.astype(jnp.float32)
    mu = jnp.mean(xf, axis=-1, keepdims=True)
    var = jnp.mean(jnp.square(xf - mu), axis=-1, keepdims=True)
    y = (xf - mu) * lax.rsqrt(var + LN_EPS)
    return (y * g.astype(jnp.float32) + b.astype(jnp.float32)).astype(x.dtype)


def axial_rope_tables(n, rot_dim):
    rows = n // GRID_W
    row = jnp.repeat(jnp.arange(rows, dtype=jnp.int32), GRID_W).astype(jnp.float32)
    col = jnp.tile(jnp.arange(GRID_W, dtype=jnp.int32), rows).astype(jnp.float32)
    nf = rot_dim // 4
    inv = ROPE_THETA ** (-jnp.arange(nf, dtype=jnp.float32) / nf)
    ang = jnp.concatenate([row[:, None] * inv, col[:, None] * inv], axis=-1)
    return jnp.cos(ang), jnp.sin(ang)


def apply_rope(x, cos, sin):
    shape = (1, cos.shape[0]) + (1,) * (x.ndim - 3) + (cos.shape[1],)
    c = cos.reshape(shape).astype(x.dtype)
    s = sin.reshape(shape).astype(x.dtype)
    x1, x2 = jnp.split(x, 2, axis=-1)
    return jnp.concatenate([x1 * c - x2 * s, x2 * c + x1 * s], axis=-1)


def attend(q, k, v):
    B, Sq, Hk, G, dk = q.shape
    dv = v.shape[-1]
    nb = Sq // Q_BLOCK
    scale = dk ** -0.5
    qb = q.reshape(B, nb, Q_BLOCK, Hk, G, dk).transpose(1, 0, 2, 3, 4, 5)

    def one_block(qblk):
        s = jnp.einsum("bqhgd,bkhd->bhgqk", qblk, k).astype(jnp.float32) * scale
        p = jax.nn.softmax(s, axis=-1).astype(v.dtype)
        return jnp.einsum("bhgqk,bkhd->bqhgd", p, v)

    o = lax.map(one_block, qb)
    return o.transpose(1, 0, 2, 3, 4, 5).reshape(B, Sq, Hk, G, dv)


def conformer_conv(u, dw_w, dw_b, ln_g, ln_b):
    a, b = jnp.split(u, 2, axis=-1)
    g = a * jax.nn.sigmoid(b)
    y = lax.conv_general_dilated(g, dw_w[:, None, :].astype(g.dtype), (1,), [(CONV_PAD, CONV_PAD)],
                                 dimension_numbers=("NWC", "WIO", "NWC"), feature_group_count=CONV_CH)
    y = layer_norm(y + dw_b, ln_g, ln_b)
    return jax.nn.silu(y)


def mixer_even(h, p, ctx):
    w_in, dw_w, dw_b, ln_g, ln_b, q_g, k_g, w_out = p
    B, S, _ = h.shape
    z = h @ w_in
    u, q, k, v = jnp.split(z, EVEN_SPLITS, axis=-1)
    conv = conformer_conv(u, dw_w, dw_b, ln_g, ln_b)
    q = rms_norm(q.reshape(B, S, GQA_KV_HEADS, GQA_GROUP, HEAD_DIM), q_g)
    k = rms_norm(k.reshape(B, S, GQA_KV_HEADS, HEAD_DIM), k_g)
    v = v.reshape(B, S, GQA_KV_HEADS, HEAD_DIM)
    if ctx is None:
        state = (k, v)
        k_all, v_all = k, v
    else:
        state = None
        cos, sin = axial_rope_tables(S, HEAD_DIM)
        q = apply_rope(q, cos, sin)
        k = apply_rope(k, cos, sin)
        k_all = jnp.concatenate([k, ctx[0]], axis=1)
        v_all = jnp.concatenate([v, ctx[1]], axis=1)
    a = attend(q, k_all, v_all).reshape(B, S, GQA_HEADS * HEAD_DIM)
    return jnp.concatenate([conv, a], axis=-1) @ w_out, state


def mla_keys(ckv, kr, w_kvb, k_g, rope):
    B, S, _ = ckv.shape
    kv = (ckv @ w_kvb).reshape(B, S, MLA_HEADS, MLA_NOPE + MLA_V)
    k_nope, v = jnp.split(kv, [MLA_NOPE], axis=-1)
    k = jnp.concatenate([k_nope, jnp.broadcast_to(kr[:, :, None, :], (B, S, MLA_HEADS, MLA_ROPE))], axis=-1)
    k = rms_norm(k, k_g)
    if rope is not None:
        k = jnp.concatenate([k[..., :MLA_NOPE], apply_rope(k[..., MLA_NOPE:], rope[0], rope[1])], axis=-1)
    return k, v


def mixer_odd(h, p, ctx):
    (w_in, dq_g, dk_g, lq1, lk1, lq2, lk2, sub_g,
     qa_g, w_qb, kva_g, w_kvb, mq_g, mk_g, w_out) = p
    B, S, _ = h.shape
    z = h @ w_in
    dq, dk, dv, qa, kva, kr = jnp.split(z, ODD_SPLITS, axis=-1)
    dq = rms_norm(dq.reshape(B, S, DIFF_HEADS, 2, DIFF_HD), dq_g)
    dk = rms_norm(dk.reshape(B, S, DIFF_HEADS, 2, DIFF_HD), dk_g)
    dv = dv.reshape(B, S, DIFF_HEADS, 2 * DIFF_HD)
    ckv = rms_norm(kva, kva_g)
    q = rms_norm((rms_norm(qa, qa_g) @ w_qb).reshape(B, S, MLA_HEADS, MLA_QK), mq_g)
    if ctx is None:
        state = (dk, dv, ckv, kr)
        k_d, v_d = dk, dv
        k_m, v_m = mla_keys(ckv, kr, w_kvb, mk_g, None)
    else:
        state = None
        c_dk, c_dv, c_ckv, c_kr = ctx
        cos, sin = axial_rope_tables(S, DIFF_HD)
        dq = apply_rope(dq, cos, sin)
        dk = apply_rope(dk, cos, sin)
        cos_r, sin_r = axial_rope_tables(S, MLA_ROPE)
        q = jnp.concatenate([q[..., :MLA_NOPE], apply_rope(q[..., MLA_NOPE:], cos_r, sin_r)], axis=-1)
        k_lat, v_lat = mla_keys(ckv, kr, w_kvb, mk_g, (cos_r, sin_r))
        k_ctx, v_ctx = mla_keys(c_ckv, c_kr, w_kvb, mk_g, None)
        k_d = jnp.concatenate([dk, c_dk], axis=1)
        v_d = jnp.concatenate([dv, c_dv], axis=1)
        k_m = jnp.concatenate([k_lat, k_ctx], axis=1)
        v_m = jnp.concatenate([v_lat, v_ctx], axis=1)
    f32 = jnp.float32
    lam = (jnp.exp(jnp.sum(lq1.astype(f32) * lk1.astype(f32))) - jnp.exp(jnp.sum(lq2.astype(f32) * lk2.astype(f32)))
           + LAMBDA_INIT)
    o1 = attend(dq[:, :, :, 0:1], k_d[:, :, :, 0], v_d)
    o2 = attend(dq[:, :, :, 1:2], k_d[:, :, :, 1], v_d)
    od = rms_norm((o1 - lam.astype(o1.dtype) * o2)[:, :, :, 0], sub_g) * (1.0 - LAMBDA_INIT)
    om = attend(q[:, :, :, None], k_m, v_m)[:, :, :, 0]
    o = jnp.concatenate([od.reshape(B, S, DIFF_W), om.reshape(B, S, MLA_HEADS * MLA_V)], axis=-1)
    return o @ w_out, state


def swiglu(h, w1, w3, w2):
    return (jax.nn.silu(h @ w1) * (h @ w3)) @ w2


def moe_swiglu(h, router_w, w1, w3, w2):
    logits = (h @ router_w).astype(jnp.float32)
    top_v, top_i = lax.top_k(logits, MOE_TOP_K)
    gates = jax.nn.softmax(top_v, axis=-1)
    combine = jnp.sum(jax.nn.one_hot(top_i, N_EXPERTS, dtype=jnp.float32) * gates[..., None], axis=-2).astype(h.dtype)
    y = jnp.zeros_like(h)
    for e in range(N_EXPERTS):
        y = y + combine[..., e:e + 1] * swiglu(h, w1[e], w3[e], w2[e])
    return y


def run_layer(l, x, cond, shared, mix_p, ffn_p, ctx):
    ada_w, ada_b, n1_g, n2_g = shared
    m = jax.nn.silu(cond) @ ada_w + ada_b
    sh1, sc1, g1, sh2, sc2, g2 = jnp.split(m[:, None, :], 6, axis=-1)
    h = rms_norm(x, n1_g) * (1.0 + sc1) + sh1
    if l % 2 == 0:
        o, state = mixer_even(h, mix_p, ctx)
    else:
        o, state = mixer_odd(h, mix_p, ctx)
    x = x + g1 * o
    h = rms_norm(x, n2_g) * (1.0 + sc2) + sh2
    f = swiglu(h, *ffn_p) if l % 2 == 0 else moe_swiglu(h, *ffn_p)
    return x + g2 * f, state


def setup_inputs(seed: int = 0) -> dict:
    key = jax.random.key(seed)
    keys = iter(jax.random.split(key, 64))
    f32 = jnp.float32

    def nrm(shape, scale=1.0):
        return jax.random.normal(next(keys), shape, f32) * scale

    def gain(n):
        return 1.0 + nrm((n,), 0.02)

    D = D_MODEL
    return {
        "x_prompt": nrm((BATCH, SEQ, D)),
        "x_sample": nrm((DEC_BATCH, DEC_SEQ, D)),
        "cache_l0_gqa_k": nrm((DEC_BATCH, PAST_LEN, GQA_KV_HEADS, HEAD_DIM)),
        "cache_l0_gqa_v": nrm((DEC_BATCH, PAST_LEN, GQA_KV_HEADS, HEAD_DIM)),
        "cache_l1_diff_k": nrm((DEC_BATCH, PAST_LEN, DIFF_HEADS, 2, DIFF_HD)),
        "cache_l1_diff_v": nrm((DEC_BATCH, PAST_LEN, DIFF_HEADS, 2 * DIFF_HD)),
        "cache_l1_mla_ckv": nrm((DEC_BATCH, PAST_LEN, MLA_KV_RANK)),
        "cache_l1_mla_krope": nrm((DEC_BATCH, PAST_LEN, MLA_ROPE)),
        "c": nrm((DEC_BATCH, D)),
        "c_ctx": nrm((D,)),
        "l0_ada_w": nrm((D, 6 * D), 0.5 * D ** -0.5),
        "l0_ada_b": nrm((6 * D,), 0.02),
        "l0_norm1_g": gain(D),
        "l0_norm2_g": gain(D),
        "l0_w_in": nrm((D, EVEN_IN), D ** -0.5),
        "l0_conv_dw_w": nrm((CONV_WIDTH, CONV_CH), CONV_WIDTH ** -0.5),
        "l0_conv_dw_b": nrm((CONV_CH,), 0.02),
        "l0_conv_ln_g": gain(CONV_CH),
        "l0_conv_ln_b": nrm((CONV_CH,), 0.02),
        "l0_q_norm_g": gain(HEAD_DIM),
        "l0_k_norm_g": gain(HEAD_DIM),
        "l0_w_out": nrm((EVEN_OUT, D), EVEN_OUT ** -0.5),
        "l0_ffn_w1": nrm((D, D_FF), D ** -0.5),
        "l0_ffn_w3": nrm((D, D_FF), D ** -0.5),
        "l0_ffn_w2": nrm((D_FF, D), D_FF ** -0.5),
        "l1_ada_w": nrm((D, 6 * D), 0.5 * D ** -0.5),
        "l1_ada_b": nrm((6 * D,), 0.02),
        "l1_norm1_g": gain(D),
        "l1_norm2_g": gain(D),
        "l1_w_in": nrm((D, ODD_IN), D ** -0.5),
        "l1_diff_q_norm_g": gain(DIFF_HD),
        "l1_diff_k_norm_g": gain(DIFF_HD),
        "l1_lambda_q1": nrm((DIFF_HD,), 0.1),
        "l1_lambda_k1": nrm((DIFF_HD,), 0.1),
        "l1_lambda_q2": nrm((DIFF_HD,), 0.1),
        "l1_lambda_k2": nrm((DIFF_HD,), 0.1),
        "l1_diff_subln_g": gain(2 * DIFF_HD),
        "l1_mla_q_a_norm_g": gain(MLA_Q_RANK),
        "l1_mla_w_qb": nrm((MLA_Q_RANK, MLA_HEADS * MLA_QK), MLA_Q_RANK ** -0.5),
        "l1_mla_kv_a_norm_g": gain(MLA_KV_RANK),
        "l1_mla_w_kvb": nrm((MLA_KV_RANK, MLA_HEADS * (MLA_NOPE + MLA_V)), MLA_KV_RANK ** -0.5),
        "l1_mla_q_norm_g": gain(MLA_QK),
        "l1_mla_k_norm_g": gain(MLA_QK),
        "l1_w_out": nrm((ODD_OUT, D), ODD_OUT ** -0.5),
        "l1_router_w": nrm((D, N_EXPERTS), D ** -0.5),
        "l1_moe_w1": nrm((N_EXPERTS, D, D_FF_EXPERT), D ** -0.5),
        "l1_moe_w3": nrm((N_EXPERTS, D, D_FF_EXPERT), D ** -0.5),
        "l1_moe_w2": nrm((N_EXPERTS, D_FF_EXPERT, D), D_FF_EXPERT ** -0.5),
    }


def reference(x_prompt, x_sample, cache_l0_gqa_k, cache_l0_gqa_v, cache_l1_diff_k, cache_l1_diff_v,
              cache_l1_mla_ckv, cache_l1_mla_krope, c, c_ctx,
              l0_ada_w, l0_ada_b, l0_norm1_g, l0_norm2_g, l0_w_in, l0_conv_dw_w, l0_conv_dw_b,
              l0_conv_ln_g, l0_conv_ln_b, l0_q_norm_g, l0_k_norm_g, l0_w_out, l0_ffn_w1, l0_ffn_w3, l0_ffn_w2,
              l1_ada_w, l1_ada_b, l1_norm1_g, l1_norm2_g, l1_w_in, l1_diff_q_norm_g, l1_diff_k_norm_g,
              l1_lambda_q1, l1_lambda_k1, l1_lambda_q2, l1_lambda_k2, l1_diff_subln_g,
              l1_mla_q_a_norm_g, l1_mla_w_qb, l1_mla_kv_a_norm_g, l1_mla_w_kvb, l1_mla_q_norm_g, l1_mla_k_norm_g,
              l1_w_out, l1_router_w, l1_moe_w1, l1_moe_w3, l1_moe_w2):
    shared = [(l0_ada_w, l0_ada_b, l0_norm1_g, l0_norm2_g),
              (l1_ada_w, l1_ada_b, l1_norm1_g, l1_norm2_g)]
    mix = [(l0_w_in, l0_conv_dw_w, l0_conv_dw_b, l0_conv_ln_g, l0_conv_ln_b, l0_q_norm_g, l0_k_norm_g, l0_w_out),
           (l1_w_in, l1_diff_q_norm_g, l1_diff_k_norm_g, l1_lambda_q1, l1_lambda_k1, l1_lambda_q2, l1_lambda_k2,
            l1_diff_subln_g, l1_mla_q_a_norm_g, l1_mla_w_qb, l1_mla_kv_a_norm_g, l1_mla_w_kvb,
            l1_mla_q_norm_g, l1_mla_k_norm_g, l1_w_out)]
    ffn = [(l0_ffn_w1, l0_ffn_w3, l0_ffn_w2),
           (l1_router_w, l1_moe_w1, l1_moe_w3, l1_moe_w2)]
    caches = [(cache_l0_gqa_k, cache_l0_gqa_v),
              (cache_l1_diff_k, cache_l1_diff_v, cache_l1_mla_ckv, cache_l1_mla_krope)]
    cond_ctx = c_ctx[None, :]
    y_prompt = x_prompt
    y_sample = x_sample
    states = []
    for l in range(DEPTH):
        y_prompt, st = run_layer(l, y_prompt, cond_ctx, shared[l], mix[l], ffn[l], None)
        y_sample, _ = run_layer(l, y_sample, c, shared[l], mix[l], ffn[l], caches[l])
        states.append(st)
    (new_l0_gqa_k, new_l0_gqa_v), (new_l1_diff_k, new_l1_diff_v, new_l1_mla_ckv, new_l1_mla_krope) = states
    return (y_prompt, y_sample, new_l0_gqa_k, new_l0_gqa_v, new_l1_diff_k, new_l1_diff_v,
            new_l1_mla_ckv, new_l1_mla_krope)
```

```python
import functools
import math

import jax
import jax.numpy as jnp
from jax import lax
from jax.experimental import pallas as pl
from jax.experimental.pallas import tpu as pltpu

F32 = jnp.float32
BF16 = jnp.bfloat16

D_MODEL = 1024
GRID_W = 64
ROPE_THETA = 10000.0
RMS_EPS = 1e-6
LN_EPS = 1e-5

CONV_CH = 512
CONV_WIDTH = 31
CONV_PAD = 15
CONV_HALO = 16
HEAD_DIM = 64
GQA_HEADS = 8
GQA_KV_HEADS = 2
GQA_GROUP = 4
EVEN_IN = 1792

DIFF_HD = 64
DIFF_HEADS = 4
DIFF_W = 512
LAMBDA_INIT = 0.8 - 0.6 * math.exp(-0.3 * 1)
MLA_HEADS = 8
MLA_NOPE = 64
MLA_ROPE = 32
MLA_QK = 96
MLA_V = 64
MLA_Q_RANK = 256
MLA_KV_RANK = 128
MLA_PAD = 128
ODD_IN_PAD = 2048

D_FF = 2816
N_EXPERTS = 8
D_FF_EXPERT = 1408

LANES = 128
VMEM_LIMIT = 56 * 1024 * 1024

TM = 512
CONV_RC = 64


def _params(sem):
    return pltpu.CompilerParams(dimension_semantics=sem, vmem_limit_bytes=VMEM_LIMIT)


def _sigmoid(x):
    return 1.0 / (1.0 + jnp.exp(-x))


def _silu(x):
    return x * _sigmoid(x)


def _dot(a, b):
    return jnp.dot(a, b, preferred_element_type=F32)


def _dot_nt(a, b):
    return lax.dot_general(a, b, (((1,), (1,)), ((), ())), preferred_element_type=F32)


def _mod_norm(x, g, sc, sh):
    y = x * lax.rsqrt(jnp.mean(x * x, axis=-1, keepdims=True) + RMS_EPS)
    return (y * g) * (1.0 + sc) + sh


def _lane(shape):
    return lax.broadcasted_iota(jnp.int32, shape, len(shape) - 1)


def _norm64(xb, gain):
    lo = _lane(xb.shape) < 64
    sq = xb * xb
    s_lo = jnp.sum(jnp.where(lo, sq, 0.0), axis=-1, keepdims=True)
    s_hi = jnp.sum(jnp.where(lo, 0.0, sq), axis=-1, keepdims=True)
    ms = jnp.where(lo, s_lo, s_hi) * (1.0 / 64.0)
    return xb * lax.rsqrt(ms + RMS_EPS) * gain


def _norm_pad(xb, gain, n_real):
    ms = jnp.sum(xb * xb, axis=-1, keepdims=True) * (1.0 / n_real)
    return xb * lax.rsqrt(ms + RMS_EPS) * gain


def _rope_blk(xb, cos_t, sin_t, half, first):
    partner = jnp.where(first, pltpu.roll(xb, LANES - half, 1), pltpu.roll(xb, half, 1))
    return xb * cos_t + partner * sin_t


def _first64(shape):
    return (_lane(shape) % 64) < 32


def _first_mla(shape):
    lane = _lane(shape)
    return (lane >= MLA_NOPE) & (lane < MLA_NOPE + MLA_ROPE // 2)


def _ada_kernel(c_ref, w_ref, b_ref, o_ref):
    s = _silu(c_ref[...])
    o_ref[...] = _dot(s.astype(BF16), w_ref[...].astype(BF16)) + b_ref[...]


def _ada(cond16, ada_w, ada_b):
    tn = 1536
    return pl.pallas_call(
        _ada_kernel,
        grid=(6 * D_MODEL // tn,),
        in_specs=[pl.BlockSpec((16, D_MODEL), lambda j: (0, 0)),
                  pl.BlockSpec((D_MODEL, tn), lambda j: (0, j)),
                  pl.BlockSpec((1, tn), lambda j: (0, j))],
        out_specs=pl.BlockSpec((16, tn), lambda j: (0, j)),
        out_shape=jax.ShapeDtypeStruct((16, 6 * D_MODEL), F32),
        compiler_params=_params(("parallel",)),
        name="ada",
    )(cond16, ada_w, ada_b.reshape(1, -1))


def _mod_index(seq, ctx):
    if ctx:
        return lambda i: (8, 0, 0)
    per = seq // TM
    return lambda i: (i // per, 0, 0)


def _row_spec(width):
    return pl.BlockSpec((TM, width), lambda i: (i, 0))


def _full_spec(shape):
    nd = len(shape)
    return pl.BlockSpec(shape, lambda *a: (0,) * nd)


def _rope_spec(seq):
    per = seq // TM
    return pl.BlockSpec((TM, LANES), lambda i: (i % per, 0))


def _proj0_kernel(*refs, rope):
    if rope:
        x_ref, mod_ref, g_ref, w_ref, qg_ref, kg_ref, c_ref, s_ref, u_ref, q_ref, k_ref, v_ref = refs
    else:
        x_ref, mod_ref, g_ref, w_ref, qg_ref, kg_ref, u_ref, q_ref, k_ref, v_ref = refs
    mod = mod_ref[0]
    h = _mod_norm(x_ref[...], g_ref[...], mod[1:2], mod[0:1])
    z = _dot(h.astype(BF16), w_ref[...])
    u_ref[...] = z[:, :2 * CONV_CH]
    base = 2 * CONV_CH
    first = _first64((TM, LANES))
    qs = []
    for b in range(GQA_HEADS * HEAD_DIM // LANES):
        xb = _norm64(z[:, base + b * LANES: base + (b + 1) * LANES], qg_ref[...])
        if rope:
            xb = _rope_blk(xb, c_ref[...], s_ref[...], HEAD_DIM // 2, first)
        qs.append((xb * (HEAD_DIM ** -0.5)).astype(BF16))
    q_ref[...] = jnp.concatenate(qs, axis=1)
    base += GQA_HEADS * HEAD_DIM
    kb = _norm64(z[:, base: base + LANES], kg_ref[...])
    if rope:
        kb = _rope_blk(kb, c_ref[...], s_ref[...], HEAD_DIM // 2, first)
    k_ref[...] = kb
    v_ref[...] = z[:, base + LANES: base + 2 * LANES]


def _proj0(x, mod, n1_g, w_in, qg, kg, rope_tabs, seq, ctx):
    n = x.shape[0]
    rope = rope_tabs is not None
    in_specs = [_row_spec(D_MODEL),
                pl.BlockSpec((1, 8, D_MODEL), _mod_index(seq, ctx)),
                _full_spec((1, D_MODEL)),
                _full_spec((D_MODEL, EVEN_IN)),
                _full_spec((1, LANES)),
                _full_spec((1, LANES))]
    args = [x, mod, n1_g, w_in, qg, kg]
    if rope:
        in_specs += [_rope_spec(seq), _rope_spec(seq)]
        args += list(rope_tabs)
    return pl.pallas_call(
        functools.partial(_proj0_kernel, rope=rope),
        grid=(n // TM,),
        in_specs=in_specs,
        out_specs=[_row_spec(2 * CONV_CH), _row_spec(512), _row_spec(LANES), _row_spec(LANES)],
        out_shape=[jax.ShapeDtypeStruct((n, 2 * CONV_CH), F32),
                   jax.ShapeDtypeStruct((n, 512), BF16),
                   jax.ShapeDtypeStruct((n, LANES), F32),
                   jax.ShapeDtypeStruct((n, LANES), F32)],
        compiler_params=_params(("parallel",)),
        name="proj0",
    )(*args)


def _conv_kernel(um_ref, up_ref, un_ref, w_ref, b_ref, lg_ref, lb_ref, o_ref, gp_ref, *, ts):
    i = pl.program_id(1)
    n_t = pl.num_programs(1)

    def glu(u):
        return u[:, :CONV_CH] * _sigmoid(u[:, CONV_CH:])

    gp_ref[CONV_HALO:CONV_HALO + ts, :] = glu(um_ref[0])
    gp_ref[0:CONV_HALO, :] = jnp.where(i > 0, glu(up_ref[0]), 0.0)
    gp_ref[CONV_HALO + ts:2 * CONV_HALO + ts, :] = jnp.where(i < n_t - 1, glu(un_ref[0]), 0.0)
    w = w_ref[...]
    off = CONV_HALO - CONV_PAD
    for c in range(ts // CONV_RC):
        acc = jnp.zeros((CONV_RC, CONV_CH), F32)
        for k in range(CONV_WIDTH):
            acc = acc + gp_ref[pl.ds(c * CONV_RC + k + off, CONV_RC), :] * w[k:k + 1, :]
        y = acc + b_ref[...]
        mu = jnp.mean(y, axis=-1, keepdims=True)
        yc = y - mu
        var = jnp.mean(yc * yc, axis=-1, keepdims=True)
        yn = yc * lax.rsqrt(var + LN_EPS) * lg_ref[...] + lb_ref[...]
        o_ref[0, c * CONV_RC:(c + 1) * CONV_RC, :] = _silu(yn).astype(BF16)


def _conv(u, dw_w, dw_b, ln_g, ln_b, ts):
    bsz, seq, _ = u.shape
    n_t = seq // ts
    hb = ts // CONV_HALO
    last_hb = seq // CONV_HALO - 1
    return pl.pallas_call(
        functools.partial(_conv_kernel, ts=ts),
        grid=(bsz, n_t),
        in_specs=[pl.BlockSpec((1, ts, 2 * CONV_CH), lambda b, i: (b, i, 0)),
                  pl.BlockSpec((1, CONV_HALO, 2 * CONV_CH), lambda b, i: (b, jnp.maximum(i * hb - 1, 0), 0)),
                  pl.BlockSpec((1, CONV_HALO, 2 * CONV_CH),
                               lambda b, i: (b, jnp.minimum((i + 1) * hb, last_hb), 0)),
                  _full_spec((CONV_WIDTH, CONV_CH)),
                  _full_spec((1, CONV_CH)),
                  _full_spec((1, CONV_CH)),
                  _full_spec((1, CONV_CH))],
        out_specs=pl.BlockSpec((1, ts, CONV_CH), lambda b, i: (b, i, 0)),
        out_shape=jax.ShapeDtypeStruct((bsz, seq, CONV_CH), BF16),
        scratch_shapes=[pltpu.VMEM((ts + 2 * CONV_HALO, CONV_CH), F32)],
        compiler_params=_params(("parallel", "parallel")),
        name="conv",
    )(u, u, u, dw_w, dw_b.reshape(1, -1), ln_g.reshape(1, -1), ln_b.reshape(1, -1))


def _softmax_pv(q, ks, vs):
    ss = [_dot_nt(q, k) for k in ks]
    m = ss[0].max(axis=-1, keepdims=True)
    for s in ss[1:]:
        m = jnp.maximum(m, s.max(axis=-1, keepdims=True))
    den = None
    o = None
    for s, v in zip(ss, vs):
        p = jnp.exp(s - m)
        d = jnp.sum(p, axis=-1, keepdims=True)
        t = _dot(p.astype(BF16), v)
        den = d if den is None else den + d
        o = t if o is None else o + t
    return o * (1.0 / den)


def _gqa_kernel(*refs, n_seg, tq):
    q_ref = refs[0]
    kv = refs[1:1 + 2 * n_seg]
    o_ref = refs[1 + 2 * n_seg]
    q = q_ref[0]
    outs = []
    for g in range(GQA_KV_HEADS):
        sl = slice(g * HEAD_DIM, (g + 1) * HEAD_DIM)
        qs = jnp.concatenate(
            [q[:, (GQA_GROUP * g + j) * HEAD_DIM:(GQA_GROUP * g + j + 1) * HEAD_DIM] for j in range(GQA_GROUP)],
            axis=0)
        ks = [kv[2 * s][0, :, sl].astype(BF16) for s in range(n_seg)]
        vs = [kv[2 * s + 1][0, :, sl].astype(BF16) for s in range(n_seg)]
        o = _softmax_pv(qs, ks, vs)
        for j in range(GQA_GROUP):
            outs.append(o[j * tq:(j + 1) * tq])
    o_ref[0] = jnp.concatenate(outs, axis=1).astype(BF16)


def _kv_specs(segs):
    specs = []
    for k, v in segs:
        specs.append(pl.BlockSpec((1,) + k.shape[1:], lambda b, i: (b, 0, 0)))
        specs.append(pl.BlockSpec((1,) + v.shape[1:], lambda b, i: (b, 0, 0)))
    return specs


def _gqa_attn(q, segs, tq):
    bsz, sq, _ = q.shape
    args = [q]
    for k, v in segs:
        args += [k, v]
    return pl.pallas_call(
        functools.partial(_gqa_kernel, n_seg=len(segs), tq=tq),
        grid=(bsz, sq // tq),
        in_specs=[pl.BlockSpec((1, tq, 512), lambda b, i: (b, i, 0))] + _kv_specs(segs),
        out_specs=pl.BlockSpec((1, tq, 512), lambda b, i: (b, i, 0)),
        out_shape=jax.ShapeDtypeStruct((bsz, sq, 512), BF16),
        compiler_params=_params(("parallel", "parallel")),
        name="gqa_attn",
    )(*args)


def _diff_kernel(*refs, n_seg):
    q_ref = refs[0]
    kv = refs[1:1 + 2 * n_seg]
    lq1, lk1, lq2, lk2, sg_ref, o_ref = refs[1 + 2 * n_seg:]
    lam = (jnp.exp(jnp.sum(lq1[...] * lk1[...], axis=-1, keepdims=True))
           - jnp.exp(jnp.sum(lq2[...] * lk2[...], axis=-1, keepdims=True)) + LAMBDA_INIT)
    q = q_ref[0]
    outs = []
    for h in range(DIFF_HEADS):
        vs = [kv[2 * s + 1][0, :, h * 2 * DIFF_HD:(h + 1) * 2 * DIFF_HD].astype(BF16) for s in range(n_seg)]
        o12 = []
        for j in range(2):
            sl = slice((2 * h + j) * DIFF_HD, (2 * h + j + 1) * DIFF_HD)
            ks = [kv[2 * s][0, :, sl].astype(BF16) for s in range(n_seg)]
            o12.append(_softmax_pv(q[:, sl], ks, vs))
        dlt = o12[0] - lam * o12[1]
        ms = jnp.mean(dlt * dlt, axis=-1, keepdims=True)
        od = dlt * lax.rsqrt(ms + RMS_EPS) * sg_ref[...] * (1.0 - LAMBDA_INIT)
        outs.append(od.astype(BF16))
    o_ref[0] = jnp.concatenate(outs, axis=1)


def _diff_attn(q, segs, lams, sub_g, tq):
    bsz, sq, _ = q.shape
    args = [q]
    for k, v in segs:
        args += [k, v]
    args += [l.reshape(1, -1) for l in lams] + [sub_g.reshape(1, -1)]
    return pl.pallas_call(
        functools.partial(_diff_kernel, n_seg=len(segs)),
        grid=(bsz, sq // tq),
        in_specs=([pl.BlockSpec((1, tq, DIFF_W), lambda b, i: (b, i, 0))] + _kv_specs(segs)
                  + [_full_spec((1, DIFF_HD))] * 4 + [_full_spec((1, 2 * DIFF_HD))]),
        out_specs=pl.BlockSpec((1, tq, DIFF_W), lambda b, i: (b, i, 0)),
        out_shape=jax.ShapeDtypeStruct((bsz, sq, DIFF_W), BF16),
        compiler_params=_params(("parallel", "parallel")),
        name="diff_attn",
    )(*args)


def _mla_kernel(*refs, n_seg):
    q_ref = refs[0]
    kv = refs[1:1 + 2 * n_seg]
    o_ref = refs[1 + 2 * n_seg]
    q = q_ref[0]
    outs = []
    for h in range(MLA_HEADS):
        ks = [kv[2 * s][0, :, h * MLA_PAD:(h + 1) * MLA_PAD] for s in range(n_seg)]
        vs = [kv[2 * s + 1][0, :, h * MLA_V:(h + 1) * MLA_V] for s in range(n_seg)]
        outs.append(_softmax_pv(q[:, h * MLA_PAD:(h + 1) * MLA_PAD], ks, vs))
    o_ref[0] = jnp.concatenate(outs, axis=1).astype(BF16)


def _mla_attn(q, segs, tq):
    bsz, sq, _ = q.shape
    args = [q]
    for k, v in segs:
        args += [k, v]
    return pl.pallas_call(
        functools.partial(_mla_kernel, n_seg=len(segs)),
        grid=(bsz, sq // tq),
        in_specs=[pl.BlockSpec((1, tq, MLA_HEADS * MLA_PAD), lambda b, i: (b, i, 0))] + _kv_specs(segs),
        out_specs=pl.BlockSpec((1, tq, MLA_HEADS * MLA_V), lambda b, i: (b, i, 0)),
        out_shape=jax.ShapeDtypeStruct((bsz, sq, MLA_HEADS * MLA_V), BF16),
        compiler_params=_params(("parallel", "parallel")),
        name="mla_attn",
    )(*args)


def _top2_combine(logits):
    lane = _lane(logits.shape)
    m1 = logits.max(axis=-1, keepdims=True)
    i1 = jnp.min(jnp.where(logits == m1, lane, LANES), axis=-1, keepdims=True)
    rest = jnp.where(lane == i1, -jnp.inf, logits)
    m2 = rest.max(axis=-1, keepdims=True)
    i2 = jnp.min(jnp.where(rest == m2, lane, LANES), axis=-1, keepdims=True)
    e = jnp.exp(m2 - m1)
    g1 = 1.0 / (1.0 + e)
    g2 = e * g1
    return jnp.where(lane == i1, g1, 0.0) + jnp.where(lane == i2, g2, 0.0)


def _post_kernel(*refs, routed):
    if routed:
        (x_ref, a_ref, b_ref, mod_ref, g_ref, woa_ref, wob_ref, rhi_ref, rlo_ref,
         w1_ref, w3_ref, w2_ref, y_ref, x1_s, h2_s, acc_s, comb_s) = refs
    else:
        (x_ref, a_ref, b_ref, mod_ref, g_ref, woa_ref, wob_ref,
         w1_ref, w3_ref, w2_ref, y_ref, x1_s, h2_s, acc_s) = refs
    j = pl.program_id(1)

    @pl.when(j == 0)
    def _():
        mod = mod_ref[0]
        o = _dot(a_ref[...], woa_ref[...]) + _dot(b_ref[...], wob_ref[...])
        x1 = x_ref[...] + mod[2:3] * o
        x1_s[...] = x1
        h2 = _mod_norm(x1, g_ref[...], mod[4:5], mod[3:4])
        hi = h2.astype(BF16)
        h2_s[...] = hi
        acc_s[...] = jnp.zeros_like(acc_s)
        if routed:
            lo = (h2 - hi.astype(F32)).astype(BF16)
            logits = _dot(hi, rhi_ref[...]) + (_dot(lo, rhi_ref[...]) + _dot(hi, rlo_ref[...]))
            logits = jnp.where(_lane(logits.shape) < N_EXPERTS, logits, -1e30)
            comb_s[...] = _top2_combine(logits)

    h2 = h2_s[...]
    act = _silu(_dot(h2, w1_ref[0])) * _dot(h2, w3_ref[0])
    if routed:
        comb = comb_s[...]
        gate = jnp.sum(jnp.where(_lane(comb.shape) == j, comb, 0.0), axis=-1, keepdims=True)
        act = act * gate
    acc_s[...] += _dot(act.astype(BF16), w2_ref[0])

    @pl.when(j == pl.num_programs(1) - 1)
    def _():
        y_ref[...] = x1_s[...] + mod_ref[0][5:6] * acc_s[...]


def _post(x, a, b, mod, n2_g, wo_a, wo_b, router, w1, w3, w2, seq, ctx):
    n = x.shape[0]
    nc, _, f = w1.shape
    routed = router is not None
    mod_idx = _mod_index(seq, ctx)
    in_specs = [pl.BlockSpec((TM, D_MODEL), lambda i, j: (i, 0)),
                pl.BlockSpec((TM, 512), lambda i, j: (i, 0)),
                pl.BlockSpec((TM, 512), lambda i, j: (i, 0)),
                pl.BlockSpec((1, 8, D_MODEL), lambda i, j: mod_idx(i)),
                _full_spec((1, D_MODEL)),
                _full_spec((512, D_MODEL)),
                _full_spec((512, D_MODEL))]
    args = [x, a, b, mod, n2_g, wo_a, wo_b]
    if routed:
        in_specs += [_full_spec((D_MODEL, LANES)), _full_spec((D_MODEL, LANES))]
        args += list(router)
    in_specs += [pl.BlockSpec((1, D_MODEL, f), lambda i, j: (j, 0, 0)),
                 pl.BlockSpec((1, D_MODEL, f), lambda i, j: (j, 0, 0)),
                 pl.BlockSpec((1, f, D_MODEL), lambda i, j: (j, 0, 0))]
    args += [w1, w3, w2]
    scratch = [pltpu.VMEM((TM, D_MODEL), F32), pltpu.VMEM((TM, D_MODEL), BF16), pltpu.VMEM((TM, D_MODEL), F32)]
    if routed:
        scratch.append(pltpu.VMEM((TM, LANES), F32))
    return pl.pallas_call(
        functools.partial(_post_kernel, routed=routed),
        grid=(n // TM, nc),
        in_specs=in_specs,
        out_specs=pl.BlockSpec((TM, D_MODEL), lambda i, j: (i, 0)),
        out_shape=jax.ShapeDtypeStruct((n, D_MODEL), F32),
        scratch_shapes=scratch,
        compiler_params=_params(("parallel", "arbitrary")),
        name="post_moe" if routed else "post_ffn",
    )(*args)


def _proj1_kernel(*refs, rope):
    if rope:
        (x_ref, mod_ref, g_ref, w_ref, dqg_ref, dkg_ref, qag_ref, wqb_ref, mqg_ref, kvg_ref,
         c_ref, s_ref, cm_ref, sm_ref, dq_ref, dk_ref, dv_ref, qm_ref, ckv_ref, kr_ref) = refs
    else:
        (x_ref, mod_ref, g_ref, w_ref, dqg_ref, dkg_ref, qag_ref, wqb_ref, mqg_ref, kvg_ref,
         dq_ref, dk_ref, dv_ref, qm_ref, ckv_ref, kr_ref) = refs
    mod = mod_ref[0]
    h = _mod_norm(x_ref[...], g_ref[...], mod[1:2], mod[0:1])
    z = _dot(h.astype(BF16), w_ref[...])
    first = _first64((TM, LANES))
    nb = DIFF_W // LANES
    dqs = []
    dks = []
    for b in range(nb):
        xq = _norm64(z[:, b * LANES:(b + 1) * LANES], dqg_ref[...])
        xk = _norm64(z[:, DIFF_W + b * LANES: DIFF_W + (b + 1) * LANES], dkg_ref[...])
        if rope:
            xq = _rope_blk(xq, c_ref[...], s_ref[...], DIFF_HD // 2, first)
            xk = _rope_blk(xk, c_ref[...], s_ref[...], DIFF_HD // 2, first)
        dqs.append((xq * (DIFF_HD ** -0.5)).astype(BF16))
        dks.append(xk)
    dq_ref[...] = jnp.concatenate(dqs, axis=1)
    dk_ref[...] = jnp.concatenate(dks, axis=1)
    dv_ref[...] = z[:, 2 * DIFF_W:3 * DIFF_W]
    base = 3 * DIFF_W
    qa = z[:, base:base + MLA_Q_RANK]
    qa = qa * lax.rsqrt(jnp.mean(qa * qa, axis=-1, keepdims=True) + RMS_EPS) * qag_ref[...]
    qm = _dot(qa.astype(BF16), wqb_ref[...])
    first_m = _first_mla((TM, LANES))
    qms = []
    for hh in range(MLA_HEADS):
        xb = _norm_pad(qm[:, hh * MLA_PAD:(hh + 1) * MLA_PAD], mqg_ref[...], MLA_QK)
        if rope:
            xb = _rope_blk(xb, cm_ref[...], sm_ref[...], MLA_ROPE // 2, first_m)
        qms.append((xb * (MLA_QK ** -0.5)).astype(BF16))
    qm_ref[...] = jnp.concatenate(qms, axis=1)
    base += MLA_Q_RANK
    kva = z[:, base:base + MLA_KV_RANK]
    ckv_ref[...] = kva * lax.rsqrt(jnp.mean(kva * kva, axis=-1, keepdims=True) + RMS_EPS) * kvg_ref[...]
    base += MLA_KV_RANK
    kr_ref[...] = z[:, base:base + MLA_ROPE]


def _proj1(x, mod, n1_g, w_in, dqg, dkg, qag, wqb, mqg, kvg, rope_tabs, seq, ctx):
    n = x.shape[0]
    rope = rope_tabs is not None
    in_specs = [_row_spec(D_MODEL),
                pl.BlockSpec((1, 8, D_MODEL), _mod_index(seq, ctx)),
                _full_spec((1, D_MODEL)),
                _full_spec((D_MODEL, ODD_IN_PAD)),
                _full_spec((1, LANES)),
                _full_spec((1, LANES)),
                _full_spec((1, MLA_Q_RANK)),
                _full_spec((MLA_Q_RANK, MLA_HEADS * MLA_PAD)),
                _full_spec((1, LANES)),
                _full_spec((1, MLA_KV_RANK))]
    args = [x, mod, n1_g, w_in, dqg, dkg, qag, wqb, mqg, kvg]
    if rope:
        in_specs += [_rope_spec(seq)] * 4
        args += list(rope_tabs)
    return pl.pallas_call(
        functools.partial(_proj1_kernel, rope=rope),
        grid=(n // TM,),
        in_specs=in_specs,
        out_specs=[_row_spec(DIFF_W), _row_spec(DIFF_W), _row_spec(DIFF_W),
                   _row_spec(MLA_HEADS * MLA_PAD), _row_spec(MLA_KV_RANK), _row_spec(MLA_ROPE)],
        out_shape=[jax.ShapeDtypeStruct((n, DIFF_W), BF16),
                   jax.ShapeDtypeStruct((n, DIFF_W), F32),
                   jax.ShapeDtypeStruct((n, DIFF_W), F32),
                   jax.ShapeDtypeStruct((n, MLA_HEADS * MLA_PAD), BF16),
                   jax.ShapeDtypeStruct((n, MLA_KV_RANK), F32),
                   jax.ShapeDtypeStruct((n, MLA_ROPE), F32)],
        compiler_params=_params(("parallel",)),
        name="proj1",
    )(*args)


def _mla_keys_kernel(*refs, rope, tm):
    if rope:
        ckv_ref, kr_ref, wk_ref, wv_ref, e_ref, g_ref, cm_ref, sm_ref, k_ref, v_ref = refs
    else:
        ckv_ref, kr_ref, wk_ref, wv_ref, e_ref, g_ref, k_ref, v_ref = refs
    ckv = ckv_ref[...].astype(BF16)
    kr = kr_ref[...]
    kr_hi = kr.astype(BF16)
    kr_lo = (kr - kr_hi.astype(F32)).astype(BF16)
    k = _dot(ckv, wk_ref[...]) + (_dot(kr_hi, e_ref[...]) + _dot(kr_lo, e_ref[...]))
    first_m = _first_mla((tm, LANES))
    ks = []
    for hh in range(MLA_HEADS):
        xb = _norm_pad(k[:, hh * MLA_PAD:(hh + 1) * MLA_PAD], g_ref[...], MLA_QK)
        if rope:
            xb = _rope_blk(xb, cm_ref[...], sm_ref[...], MLA_ROPE // 2, first_m)
        ks.append(xb.astype(BF16))
    k_ref[...] = jnp.concatenate(ks, axis=1)
    v_ref[...] = _dot(ckv, wv_ref[...]).astype(BF16)


def _mla_keys(ckv, kr, wk, wv, e_mat, mkg, rope_tabs, seq, tm):
    n = ckv.shape[0]
    rope = rope_tabs is not None
    row = lambda w: pl.BlockSpec((tm, w), lambda i: (i, 0))
    in_specs = [row(MLA_KV_RANK), row(MLA_ROPE),
                _full_spec((MLA_KV_RANK, MLA_HEADS * MLA_PAD)),
                _full_spec((MLA_KV_RANK, MLA_HEADS * MLA_V)),
                _full_spec((MLA_ROPE, MLA_HEADS * MLA_PAD)),
                _full_spec((1, LANES))]
    args = [ckv, kr, wk, wv, e_mat, mkg]
    if rope:
        per = seq // tm
        in_specs += [pl.BlockSpec((tm, LANES), lambda i: (i % per, 0))] * 2
        args += list(rope_tabs)
    return pl.pallas_call(
        functools.partial(_mla_keys_kernel, rope=rope, tm=tm),
        grid=(n // tm,),
        in_specs=in_specs,
        out_specs=[row(MLA_HEADS * MLA_PAD), row(MLA_HEADS * MLA_V)],
        out_shape=[jax.ShapeDtypeStruct((n, MLA_HEADS * MLA_PAD), BF16),
                   jax.ShapeDtypeStruct((n, MLA_HEADS * MLA_V), BF16)],
        compiler_params=_params(("parallel",)),
        name="mla_keys",
    )(*args)


def _axial_tables(n, rot_dim):
    rows = n // GRID_W
    row = jnp.repeat(jnp.arange(rows, dtype=jnp.int32), GRID_W).astype(F32)
    col = jnp.tile(jnp.arange(GRID_W, dtype=jnp.int32), rows).astype(F32)
    nf = rot_dim // 4
    inv = ROPE_THETA ** (-jnp.arange(nf, dtype=F32) / nf)
    ang = jnp.concatenate([row[:, None] * inv, col[:, None] * inv], axis=-1)
    return jnp.cos(ang), jnp.sin(ang)


def _rope_tabs64(n):
    cos, sin = _axial_tables(n, HEAD_DIM)
    return (jnp.concatenate([cos, cos, cos, cos], axis=-1),
            jnp.concatenate([-sin, sin, -sin, sin], axis=-1))


def _rope_tabs_mla(n):
    cos, sin = _axial_tables(n, MLA_ROPE)
    one = jnp.ones((n, MLA_NOPE), F32)
    zero = jnp.zeros((n, MLA_NOPE), F32)
    pad = jnp.zeros((n, MLA_PAD - MLA_QK), F32)
    return (jnp.concatenate([one, cos, cos, pad], axis=-1),
            jnp.concatenate([zero, -sin, sin, pad], axis=-1))


def _tile_gain(g, reps):
    return jnp.tile(g, reps).reshape(1, -1)


def _pad_gain_mla(g):
    return jnp.concatenate([g, jnp.zeros((MLA_PAD - MLA_QK,), F32)]).reshape(1, -1)


def kernel(x_prompt, x_sample, cache_l0_gqa_k, cache_l0_gqa_v, cache_l1_diff_k, cache_l1_diff_v, cache_l1_mla_ckv, cache_l1_mla_krope, c, c_ctx, l0_ada_w, l0_ada_b, l0_norm1_g, l0_norm2_g, l0_w_in, l0_conv_dw_w, l0_conv_dw_b, l0_conv_ln_g, l0_conv_ln_b, l0_q_norm_g, l0_k_norm_g, l0_w_out, l0_ffn_w1, l0_ffn_w3, l0_ffn_w2, l1_ada_w, l1_ada_b, l1_norm1_g, l1_norm2_g, l1_w_in, l1_diff_q_norm_g, l1_diff_k_norm_g, l1_lambda_q1, l1_lambda_k1, l1_lambda_q2, l1_lambda_k2, l1_diff_subln_g, l1_mla_q_a_norm_g, l1_mla_w_qb, l1_mla_kv_a_norm_g, l1_mla_w_kvb, l1_mla_q_norm_g, l1_mla_k_norm_g, l1_w_out, l1_router_w, l1_moe_w1, l1_moe_w3, l1_moe_w2):
    pb, ps, _ = x_prompt.shape
    sb, ss, _ = x_sample.shape
    past = cache_l0_gqa_k.shape[1]
    n_p = pb * ps
    n_s = sb * ss

    cond16 = jnp.concatenate([c, c_ctx[None, :], jnp.zeros((16 - sb - 1, D_MODEL), F32)], axis=0)

    def mod_table(ada_w, ada_b):
        m = _ada(cond16, ada_w, ada_b).reshape(16, 6, D_MODEL)
        return jnp.pad(m, ((0, 0), (0, 2), (0, 0)))

    mod0 = mod_table(l0_ada_w, l0_ada_b)
    mod1 = mod_table(l1_ada_w, l1_ada_b)

    tabs64 = _rope_tabs64(ss)
    tabs_mla = _rope_tabs_mla(ss)

    xp = x_prompt.reshape(n_p, D_MODEL)
    xs = x_sample.reshape(n_s, D_MODEL)

    w_in0 = l0_w_in.astype(BF16)
    qg0 = _tile_gain(l0_q_norm_g, 2)
    kg0 = _tile_gain(l0_k_norm_g, 2)
    n1g0 = l0_norm1_g.reshape(1, -1)
    n2g0 = l0_norm2_g.reshape(1, -1)
    wo0 = l0_w_out.astype(BF16)
    nc0 = D_FF // D_FF_EXPERT
    w1_0 = l0_ffn_w1.astype(BF16).reshape(D_MODEL, nc0, D_FF_EXPERT).transpose(1, 0, 2)
    w3_0 = l0_ffn_w3.astype(BF16).reshape(D_MODEL, nc0, D_FF_EXPERT).transpose(1, 0, 2)
    w2_0 = l0_ffn_w2.astype(BF16).reshape(nc0, D_FF_EXPERT, D_MODEL)

    def layer0(x, seq, bsz, ctx, tabs, cache):
        u, q, k, v = _proj0(x, mod0, n1g0, w_in0, qg0, kg0, tabs, seq, ctx)
        cv = _conv(u.reshape(bsz, seq, 2 * CONV_CH), l0_conv_dw_w, l0_conv_dw_b, l0_conv_ln_g, l0_conv_ln_b,
                   ts=min(seq, 512))
        k3 = k.reshape(bsz, seq, LANES)
        v3 = v.reshape(bsz, seq, LANES)
        segs = [(k3, v3)]
        if cache is not None:
            segs.append(cache)
        at = _gqa_attn(q.reshape(bsz, seq, 512), segs, tq=256)
        y = _post(x, cv.reshape(-1, CONV_CH), at.reshape(-1, 512), mod0, n2g0, wo0[:CONV_CH], wo0[CONV_CH:],
                  None, w1_0, w3_0, w2_0, seq, ctx)
        return y, k, v

    yp0, k0, v0 = layer0(xp, ps, pb, True, None, None)
    cache0 = (cache_l0_gqa_k.reshape(sb, past, LANES), cache_l0_gqa_v.reshape(sb, past, LANES))
    ys0, _, _ = layer0(xs, ss, sb, False, tabs64, cache0)

    w_in1 = jnp.pad(l1_w_in, ((0, 0), (0, ODD_IN_PAD - l1_w_in.shape[1]))).astype(BF16)
    dqg = _tile_gain(l1_diff_q_norm_g, 2)
    dkg = _tile_gain(l1_diff_k_norm_g, 2)
    qag = l1_mla_q_a_norm_g.reshape(1, -1)
    kvg = l1_mla_kv_a_norm_g.reshape(1, -1)
    wqb = jnp.pad(l1_mla_w_qb.reshape(MLA_Q_RANK, MLA_HEADS, MLA_QK),
                  ((0, 0), (0, 0), (0, MLA_PAD - MLA_QK))).reshape(MLA_Q_RANK, -1).astype(BF16)
    mqg = _pad_gain_mla(l1_mla_q_norm_g)
    mkg = _pad_gain_mla(l1_mla_k_norm_g)
    wkvb = l1_mla_w_kvb.reshape(MLA_KV_RANK, MLA_HEADS, MLA_NOPE + MLA_V)
    wk = jnp.pad(wkvb[:, :, :MLA_NOPE], ((0, 0), (0, 0), (0, MLA_PAD - MLA_NOPE))).reshape(MLA_KV_RANK, -1)
    wk = wk.astype(BF16)
    wv = wkvb[:, :, MLA_NOPE:].reshape(MLA_KV_RANK, -1).astype(BF16)
    e_one = jnp.concatenate([jnp.zeros((MLA_ROPE, MLA_NOPE), F32), jnp.eye(MLA_ROPE, dtype=F32),
                             jnp.zeros((MLA_ROPE, MLA_PAD - MLA_QK), F32)], axis=1)
    e_mat = jnp.tile(e_one, (1, MLA_HEADS)).astype(BF16)
    n1g1 = l1_norm1_g.reshape(1, -1)
    n2g1 = l1_norm2_g.reshape(1, -1)
    wo1 = l1_w_out.astype(BF16)
    rw = jnp.pad(l1_router_w, ((0, 0), (0, LANES - N_EXPERTS)))
    rw_hi = rw.astype(BF16)
    rw_lo = (rw - rw_hi.astype(F32)).astype(BF16)
    mw1 = l1_moe_w1.astype(BF16)
    mw3 = l1_moe_w3.astype(BF16)
    mw2 = l1_moe_w2.astype(BF16)
    lams = (l1_lambda_q1, l1_lambda_k1, l1_lambda_q2, l1_lambda_k2)

    def layer1(x, seq, bsz, ctx, tabs, caches):
        rope_tabs = None if tabs is None else (tabs[0][0], tabs[0][1], tabs[1][0], tabs[1][1])
        dq, dk, dv, qm, ckv, kr = _proj1(x, mod1, n1g1, w_in1, dqg, dkg, qag, wqb, mqg, kvg, rope_tabs, seq, ctx)
        km, vm = _mla_keys(ckv, kr, wk, wv, e_mat, mkg, None if tabs is None else tabs[1], seq, TM)
        dsegs = [(dk.reshape(bsz, seq, DIFF_W), dv.reshape(bsz, seq, DIFF_W))]
        msegs = [(km.reshape(bsz, seq, -1), vm.reshape(bsz, seq, -1))]
        if caches is not None:
            c_dk, c_dv, c_ckv, c_kr = caches
            dsegs.append((c_dk.reshape(sb, past, DIFF_W), c_dv.reshape(sb, past, DIFF_W)))
            ckm, cvm = _mla_keys(c_ckv.reshape(-1, MLA_KV_RANK), c_kr.reshape(-1, MLA_ROPE), wk, wv, e_mat, mkg,
                                 None, past, past)
            msegs.append((ckm.reshape(sb, past, -1), cvm.reshape(sb, past, -1)))
        od = _diff_attn(dq.reshape(bsz, seq, DIFF_W), dsegs, lams, l1_diff_subln_g, tq=256)
        om = _mla_attn(qm.reshape(bsz, seq, -1), msegs, tq=256)
        y = _post(x, od.reshape(-1, DIFF_W), om.reshape(-1, 512), mod1, n2g1, wo1[:DIFF_W], wo1[DIFF_W:],
                  (rw_hi, rw_lo), mw1, mw3, mw2, seq, ctx)
        return y, dk, dv, ckv, kr

    yp1, dk1, dv1, ckv1, kr1 = layer1(yp0, ps, pb, True, None, None)
    ys1, _, _, _, _ = layer1(ys0, ss, sb, False, (tabs64, tabs_mla),
                             (cache_l1_diff_k, cache_l1_diff_v, cache_l1_mla_ckv, cache_l1_mla_krope))

    return (yp1.reshape(pb, ps, D_MODEL), ys1.reshape(sb, ss, D_MODEL),
            k0.reshape(pb, ps, GQA_KV_HEADS, HEAD_DIM), v0.reshape(pb, ps, GQA_KV_HEADS, HEAD_DIM),
            dk1.reshape(pb, ps, DIFF_HEADS, 2, DIFF_HD), dv1.reshape(pb, ps, DIFF_HEADS, 2 * DIFF_HD),
            ckv1.reshape(pb, ps, MLA_KV_RANK), kr1.reshape(pb, ps, MLA_ROPE))
```

```python
import functools
import math

import jax
import jax.numpy as jnp
from jax import lax
from jax.experimental import pallas as pl
from jax.experimental.pallas import tpu as pltpu

F32 = jnp.float32
BF16 = jnp.bfloat16

D_MODEL = 1024
GRID_W = 64
ROPE_THETA = 10000.0
RMS_EPS = 1e-6
LN_EPS = 1e-5

CONV_CH = 512
CONV_WIDTH = 31
CONV_PAD = 15
CONV_HALO = 16
HEAD_DIM = 64
GQA_HEADS = 8
GQA_KV_HEADS = 2
GQA_GROUP = 4
EVEN_IN = 1792

DIFF_HD = 64
DIFF_HEADS = 4
DIFF_W = 512
LAMBDA_INIT = 0.8 - 0.6 * math.exp(-0.3 * 1)
MLA_HEADS = 8
MLA_NOPE = 64
MLA_ROPE = 32
MLA_QK = 96
MLA_V = 64
MLA_Q_RANK = 256
MLA_KV_RANK = 128
MLA_PAD = 128
ODD_IN_PAD = 2048

D_FF = 2816
N_EXPERTS = 8
D_FF_EXPERT = 1408

LANES = 128
VMEM_LIMIT = 56 * 1024 * 1024

TM = 512
CONV_RC = 64


def _params(sem):
    return pltpu.CompilerParams(dimension_semantics=sem, vmem_limit_bytes=VMEM_LIMIT)


def _sigmoid(x):
    return 1.0 / (1.0 + jnp.exp(-x))


def _silu(x):
    return x * _sigmoid(x)


def _dot(a, b):
    return jnp.dot(a, b, preferred_element_type=F32)


def _dot_nt(a, b):
    return lax.dot_general(a, b, (((1,), (1,)), ((), ())), preferred_element_type=F32)


def _mod_norm(x, g, sc, sh):
    y = x * lax.rsqrt(jnp.mean(x * x, axis=-1, keepdims=True) + RMS_EPS)
    return (y * g) * (1.0 + sc) + sh


def _lane(shape):
    return lax.broadcasted_iota(jnp.int32, shape, len(shape) - 1)


def _norm64(xb, gain):
    lo = _lane(xb.shape) < 64
    sq = xb * xb
    s_lo = jnp.sum(jnp.where(lo, sq, 0.0), axis=-1, keepdims=True)
    s_hi = jnp.sum(jnp.where(lo, 0.0, sq), axis=-1, keepdims=True)
    ms = jnp.where(lo, s_lo, s_hi) * (1.0 / 64.0)
    return xb * lax.rsqrt(ms + RMS_EPS) * gain


def _norm_pad(xb, gain, n_real):
    ms = jnp.sum(xb * xb, axis=-1, keepdims=True) * (1.0 / n_real)
    return xb * lax.rsqrt(ms + RMS_EPS) * gain


def _rope_blk(xb, cos_t, sin_t, half, first):
    partner = jnp.where(first, pltpu.roll(xb, LANES - half, 1), pltpu.roll(xb, half, 1))
    return xb * cos_t + partner * sin_t


def _first64(shape):
    return (_lane(shape) % 64) < 32


def _first_mla(shape):
    lane = _lane(shape)
    return (lane >= MLA_NOPE) & (lane < MLA_NOPE + MLA_ROPE // 2)


def _ada_kernel(c_ref, w_ref, b_ref, o_ref):
    s = _silu(c_ref[...])
    o_ref[...] = _dot(s.astype(BF16), w_ref[...].astype(BF16)) + b_ref[...]


def _ada(cond16, ada_w, ada_b):
    tn = 1536
    return pl.pallas_call(
        _ada_kernel,
        grid=(6 * D_MODEL // tn,),
        in_specs=[pl.BlockSpec((16, D_MODEL), lambda j: (0, 0)),
                  pl.BlockSpec((D_MODEL, tn), lambda j: (0, j)),
                  pl.BlockSpec((1, tn), lambda j: (0, j))],
        out_specs=pl.BlockSpec((16, tn), lambda j: (0, j)),
        out_shape=jax.ShapeDtypeStruct((16, 6 * D_MODEL), F32),
        compiler_params=_params(("parallel",)),
        name="ada",
    )(cond16, ada_w, ada_b.reshape(1, -1))


def _mod_index(seq, ctx):
    if ctx:
        return lambda i: (8, 0, 0)
    per = seq // TM
    return lambda i: (i // per, 0, 0)


def _row_spec(width):
    return pl.BlockSpec((TM, width), lambda i: (i, 0))


def _full_spec(shape):
    nd = len(shape)
    return pl.BlockSpec(shape, lambda *a: (0,) * nd)


def _rope_spec(seq):
    per = seq // TM
    return pl.BlockSpec((TM, LANES), lambda i: (i % per, 0))


def _proj0_kernel(*refs, rope):
    if rope:
        x_ref, mod_ref, g_ref, w_ref, qg_ref, kg_ref, c_ref, s_ref, u_ref, q_ref, k_ref, v_ref = refs
    else:
        x_ref, mod_ref, g_ref, w_ref, qg_ref, kg_ref, u_ref, q_ref, k_ref, v_ref = refs
    mod = mod_ref[0]
    h = _mod_norm(x_ref[...], g_ref[...], mod[1:2], mod[0:1])
    z = _dot(h.astype(BF16), w_ref[...])
    u_ref[...] = z[:, :2 * CONV_CH]
    base = 2 * CONV_CH
    first = _first64((TM, LANES))
    qs = []
    for b in range(GQA_HEADS * HEAD_DIM // LANES):
        xb = _norm64(z[:, base + b * LANES: base + (b + 1) * LANES], qg_ref[...])
        if rope:
            xb = _rope_blk(xb, c_ref[...], s_ref[...], HEAD_DIM // 2, first)
        qs.append((xb * (HEAD_DIM ** -0.5)).astype(BF16))
    q_ref[...] = jnp.concatenate(qs, axis=1)
    base += GQA_HEADS * HEAD_DIM
    kb = _norm64(z[:, base: base + LANES], kg_ref[...])
    if rope:
        kb = _rope_blk(kb, c_ref[...], s_ref[...], HEAD_DIM // 2, first)
    k_ref[...] = kb
    v_ref[...] = z[:, base + LANES: base + 2 * LANES]


def _proj0(x, mod, n1_g, w_in, qg, kg, rope_tabs, seq, ctx):
    n = x.shape[0]
    rope = rope_tabs is not None
    in_specs = [_row_spec(D_MODEL),
                pl.BlockSpec((1, 8, D_MODEL), _mod_index(seq, ctx)),
                _full_spec((1, D_MODEL)),
                _full_spec((D_MODEL, EVEN_IN)),
                _full_spec((1, LANES)),
                _full_spec((1, LANES))]
    args = [x, mod, n1_g, w_in, qg, kg]
    if rope:
        in_specs += [_rope_spec(seq), _rope_spec(seq)]
        args += list(rope_tabs)
    return pl.pallas_call(
        functools.partial(_proj0_kernel, rope=rope),
        grid=(n // TM,),
        in_specs=in_specs,
        out_specs=[_row_spec(2 * CONV_CH), _row_spec(512), _row_spec(LANES), _row_spec(LANES)],
        out_shape=[jax.ShapeDtypeStruct((n, 2 * CONV_CH), F32),
                   jax.ShapeDtypeStruct((n, 512), BF16),
                   jax.ShapeDtypeStruct((n, LANES), F32),
                   jax.ShapeDtypeStruct((n, LANES), F32)],
        compiler_params=_params(("parallel",)),
        name="proj0",
    )(*args)


def _conv_kernel(um_ref, up_ref, un_ref, w_ref, b_ref, lg_ref, lb_ref, o_ref, gp_ref, *, ts):
    i = pl.program_id(1)
    n_t = pl.num_programs(1)

    def glu(u):
        return u[:, :CONV_CH] * _sigmoid(u[:, CONV_CH:])

    gp_ref[CONV_HALO:CONV_HALO + ts, :] = glu(um_ref[0])
    gp_ref[0:CONV_HALO, :] = jnp.where(i > 0, glu(up_ref[0]), 0.0)
    gp_ref[CONV_HALO + ts:2 * CONV_HALO + ts, :] = jnp.where(i < n_t - 1, glu(un_ref[0]), 0.0)
    w = w_ref[...]
    off = CONV_HALO - CONV_PAD
    for c in range(ts // CONV_RC):
        acc = jnp.zeros((CONV_RC, CONV_CH), F32)
        for k in range(CONV_WIDTH):
            acc = acc + gp_ref[pl.ds(c * CONV_RC + k + off, CONV_RC), :] * w[k:k + 1, :]
        y = acc + b_ref[...]
        mu = jnp.mean(y, axis=-1, keepdims=True)
        yc = y - mu
        var = jnp.mean(yc * yc, axis=-1, keepdims=True)
        yn = yc * lax.rsqrt(var + LN_EPS) * lg_ref[...] + lb_ref[...]
        o_ref[0, c * CONV_RC:(c + 1) * CONV_RC, :] = _silu(yn).astype(BF16)


def _conv(u, dw_w, dw_b, ln_g, ln_b, ts):
    bsz, seq, _ = u.shape
    n_t = seq // ts
    hb = ts // CONV_HALO
    last_hb = seq // CONV_HALO - 1
    return pl.pallas_call(
        functools.partial(_conv_kernel, ts=ts),
        grid=(bsz, n_t),
        in_specs=[pl.BlockSpec((1, ts, 2 * CONV_CH), lambda b, i: (b, i, 0)),
                  pl.BlockSpec((1, CONV_HALO, 2 * CONV_CH), lambda b, i: (b, jnp.maximum(i * hb - 1, 0), 0)),
                  pl.BlockSpec((1, CONV_HALO, 2 * CONV_CH),
                               lambda b, i: (b, jnp.minimum((i + 1) * hb, last_hb), 0)),
                  _full_spec((CONV_WIDTH, CONV_CH)),
                  _full_spec((1, CONV_CH)),
                  _full_spec((1, CONV_CH)),
                  _full_spec((1, CONV_CH))],
        out_specs=pl.BlockSpec((1, ts, CONV_CH), lambda b, i: (b, i, 0)),
        out_shape=jax.ShapeDtypeStruct((bsz, seq, CONV_CH), BF16),
        scratch_shapes=[pltpu.VMEM((ts + 2 * CONV_HALO, CONV_CH), F32)],
        compiler_params=_params(("parallel", "parallel")),
        name="conv",
    )(u, u, u, dw_w, dw_b.reshape(1, -1), ln_g.reshape(1, -1), ln_b.reshape(1, -1))


def _softmax_pv(q, ks, vs):
    ss = [_dot_nt(q, k) for k in ks]
    m = ss[0].max(axis=-1, keepdims=True)
    for s in ss[1:]:
        m = jnp.maximum(m, s.max(axis=-1, keepdims=True))
    den = None
    o = None
    for s, v in zip(ss, vs):
        p = jnp.exp(s - m)
        d = jnp.sum(p, axis=-1, keepdims=True)
        t = _dot(p.astype(BF16), v)
        den = d if den is None else den + d
        o = t if o is None else o + t
    return o * (1.0 / den)


def _gqa_kernel(*refs, n_seg, tq):
    q_ref = refs[0]
    kv = refs[1:1 + 2 * n_seg]
    o_ref = refs[1 + 2 * n_seg]
    q = q_ref[0]
    outs = []
    for g in range(GQA_KV_HEADS):
        sl = slice(g * HEAD_DIM, (g + 1) * HEAD_DIM)
        qs = jnp.concatenate(
            [q[:, (GQA_GROUP * g + j) * HEAD_DIM:(GQA_GROUP * g + j + 1) * HEAD_DIM] for j in range(GQA_GROUP)],
            axis=0)
        ks = [kv[2 * s][0, :, sl].astype(BF16) for s in range(n_seg)]
        vs = [kv[2 * s + 1][0, :, sl].astype(BF16) for s in range(n_seg)]
        o = _softmax_pv(qs, ks, vs)
        for j in range(GQA_GROUP):
            outs.append(o[j * tq:(j + 1) * tq])
    o_ref[0] = jnp.concatenate(outs, axis=1).astype(BF16)


def _kv_specs(segs):
    specs = []
    for k, v in segs:
        specs.append(pl.BlockSpec((1,) + k.shape[1:], lambda b, i: (b, 0, 0)))
        specs.append(pl.BlockSpec((1,) + v.shape[1:], lambda b, i: (b, 0, 0)))
    return specs


def _gqa_attn(q, segs, tq):
    bsz, sq, _ = q.shape
    args = [q]
    for k, v in segs:
        args += [k, v]
    return pl.pallas_call(
        functools.partial(_gqa_kernel, n_seg=len(segs), tq=tq),
        grid=(bsz, sq // tq),
        in_specs=[pl.BlockSpec((1, tq, 512), lambda b, i: (b, i, 0))] + _kv_specs(segs),
        out_specs=pl.BlockSpec((1, tq, 512), lambda b, i: (b, i, 0)),
        out_shape=jax.ShapeDtypeStruct((bsz, sq, 512), BF16),
        compiler_params=_params(("parallel", "parallel")),
        name="gqa_attn",
    )(*args)


def _diff_kernel(*refs, n_seg):
    q_ref = refs[0]
    kv = refs[1:1 + 2 * n_seg]
    lq1, lk1, lq2, lk2, sg_ref, o_ref = refs[1 + 2 * n_seg:]
    lam = (jnp.exp(jnp.sum(lq1[...] * lk1[...], axis=-1, keepdims=True))
           - jnp.exp(jnp.sum(lq2[...] * lk2[...], axis=-1, keepdims=True)) + LAMBDA_INIT)
    q = q_ref[0]
    outs = []
    for h in range(DIFF_HEADS):
        vs = [kv[2 * s + 1][0, :, h * 2 * DIFF_HD:(h + 1) * 2 * DIFF_HD].astype(BF16) for s in range(n_seg)]
        o12 = []
        for j in range(2):
            sl = slice((2 * h + j) * DIFF_HD, (2 * h + j + 1) * DIFF_HD)
            ks = [kv[2 * s][0, :, sl].astype(BF16) for s in range(n_seg)]
            o12.append(_softmax_pv(q[:, sl], ks, vs))
        dlt = o12[0] - lam * o12[1]
        ms = jnp.mean(dlt * dlt, axis=-1, keepdims=True)
        od = dlt * lax.rsqrt(ms + RMS_EPS) * sg_ref[...] * (1.0 - LAMBDA_INIT)
        outs.append(od.astype(BF16))
    o_ref[0] = jnp.concatenate(outs, axis=1)


def _diff_attn(q, segs, lams, sub_g, tq):
    bsz, sq, _ = q.shape
    args = [q]
    for k, v in segs:
        args += [k, v]
    args += [l.reshape(1, -1) for l in lams] + [sub_g.reshape(1, -1)]
    return pl.pallas_call(
        functools.partial(_diff_kernel, n_seg=len(segs)),
        grid=(bsz, sq // tq),
        in_specs=([pl.BlockSpec((1, tq, DIFF_W), lambda b, i: (b, i, 0))] + _kv_specs(segs)
                  + [_full_spec((1, DIFF_HD))] * 4 + [_full_spec((1, 2 * DIFF_HD))]),
        out_specs=pl.BlockSpec((1, tq, DIFF_W), lambda b, i: (b, i, 0)),
        out_shape=jax.ShapeDtypeStruct((bsz, sq, DIFF_W), BF16),
        compiler_params=_params(("parallel", "parallel")),
        name="diff_attn",
    )(*args)


def _mla_kernel(*refs, n_seg):
    q_ref = refs[0]
    kv = refs[1:1 + 2 * n_seg]
    o_ref = refs[1 + 2 * n_seg]
    q = q_ref[0]
    outs = []
    for h in range(MLA_HEADS):
        ks = [kv[2 * s][0, :, h * MLA_PAD:(h + 1) * MLA_PAD] for s in range(n_seg)]
        vs = [kv[2 * s + 1][0, :, h * MLA_V:(h + 1) * MLA_V] for s in range(n_seg)]
        outs.append(_softmax_pv(q[:, h * MLA_PAD:(h + 1) * MLA_PAD], ks, vs))
    o_ref[0] = jnp.concatenate(outs, axis=1).astype(BF16)


def _mla_attn(q, segs, tq):
    bsz, sq, _ = q.shape
    args = [q]
    for k, v in segs:
        args += [k, v]
    return pl.pallas_call(
        functools.partial(_mla_kernel, n_seg=len(segs)),
        grid=(bsz, sq // tq),
        in_specs=[pl.BlockSpec((1, tq, MLA_HEADS * MLA_PAD), lambda b, i: (b, i, 0))] + _kv_specs(segs),
        out_specs=pl.BlockSpec((1, tq, MLA_HEADS * MLA_V), lambda b, i: (b, i, 0)),
        out_shape=jax.ShapeDtypeStruct((bsz, sq, MLA_HEADS * MLA_V), BF16),
        compiler_params=_params(("parallel", "parallel")),
        name="mla_attn",
    )(*args)


def _post_kernel(x_ref, a_ref, b_ref, mod_ref, g_ref, woa_ref, wob_ref,
                 w1_ref, w3_ref, w2_ref, y_ref, x1_s, h2_s, acc_s):
    j = pl.program_id(1)

    @pl.when(j == 0)
    def _():
        mod = mod_ref[0]
        o = _dot(a_ref[...], woa_ref[...]) + _dot(b_ref[...], wob_ref[...])
        x1 = x_ref[...] + mod[2:3] * o
        x1_s[...] = x1
        h2_s[...] = _mod_norm(x1, g_ref[...], mod[4:5], mod[3:4]).astype(BF16)
        acc_s[...] = jnp.zeros_like(acc_s)

    h2 = h2_s[...]
    act = _silu(_dot(h2, w1_ref[0])) * _dot(h2, w3_ref[0])
    acc_s[...] += _dot(act.astype(BF16), w2_ref[0])

    @pl.when(j == pl.num_programs(1) - 1)
    def _():
        y_ref[...] = x1_s[...] + mod_ref[0][5:6] * acc_s[...]


def _post(x, a, b, mod, n2_g, wo_a, wo_b, w1, w3, w2, seq, ctx):
    n = x.shape[0]
    nc, _, f = w1.shape
    mod_idx = _mod_index(seq, ctx)
    return pl.pallas_call(
        _post_kernel,
        grid=(n // TM, nc),
        in_specs=[pl.BlockSpec((TM, D_MODEL), lambda i, j: (i, 0)),
                  pl.BlockSpec((TM, 512), lambda i, j: (i, 0)),
                  pl.BlockSpec((TM, 512), lambda i, j: (i, 0)),
                  pl.BlockSpec((1, 8, D_MODEL), lambda i, j: mod_idx(i)),
                  _full_spec((1, D_MODEL)),
                  _full_spec((512, D_MODEL)),
                  _full_spec((512, D_MODEL)),
                  pl.BlockSpec((1, D_MODEL, f), lambda i, j: (j, 0, 0)),
                  pl.BlockSpec((1, D_MODEL, f), lambda i, j: (j, 0, 0)),
                  pl.BlockSpec((1, f, D_MODEL), lambda i, j: (j, 0, 0))],
        out_specs=pl.BlockSpec((TM, D_MODEL), lambda i, j: (i, 0)),
        out_shape=jax.ShapeDtypeStruct((n, D_MODEL), F32),
        scratch_shapes=[pltpu.VMEM((TM, D_MODEL), F32), pltpu.VMEM((TM, D_MODEL), BF16),
                        pltpu.VMEM((TM, D_MODEL), F32)],
        compiler_params=_params(("parallel", "arbitrary")),
        name="post_ffn",
    )(x, a, b, mod, n2_g, wo_a, wo_b, w1, w3, w2)


R_E1, R_E2, R_RANK1, R_RANK2, R_G1, R_G2 = range(6)


def _route_kernel(x_ref, a_ref, b_ref, mod_ref, g_ref, woa_ref, wob_ref, rhi_ref, rlo_ref,
                  x1_ref, h2_ref, route_ref, cnt_ref, carry_s):
    i = pl.program_id(0)

    @pl.when(i == 0)
    def _():
        carry_s[...] = jnp.zeros_like(carry_s)

    mod = mod_ref[0]
    o = _dot(a_ref[...], woa_ref[...]) + _dot(b_ref[...], wob_ref[...])
    x1 = x_ref[...] + mod[2:3] * o
    x1_ref[...] = x1
    h2 = _mod_norm(x1, g_ref[...], mod[4:5], mod[3:4])
    h2_ref[...] = h2
    hi = h2.astype(BF16)
    lo = (h2 - hi.astype(F32)).astype(BF16)
    logits = _dot(hi, rhi_ref[...]) + (_dot(lo, rhi_ref[...]) + _dot(hi, rlo_ref[...]))
    lane = _lane(logits.shape)
    logits = jnp.where(lane < N_EXPERTS, logits, -1e30)
    m1 = logits.max(axis=-1, keepdims=True)
    i1 = jnp.min(jnp.where(logits == m1, lane, LANES), axis=-1, keepdims=True)
    rest = jnp.where(lane == i1, -jnp.inf, logits)
    m2 = rest.max(axis=-1, keepdims=True)
    i2 = jnp.min(jnp.where(rest == m2, lane, LANES), axis=-1, keepdims=True)
    e = jnp.exp(m2 - m1)
    g1 = 1.0 / (1.0 + e)
    g2 = e * g1
    sel1 = lane == i1
    sel2 = lane == i2
    onehot = jnp.where(sel1, 1.0, 0.0) + jnp.where(sel2, 1.0, 0.0)
    rr = lax.broadcasted_iota(jnp.int32, (TM, TM), 0)
    cc = lax.broadcasted_iota(jnp.int32, (TM, TM), 1)
    tri = jnp.where(cc < rr, 1.0, 0.0).astype(BF16)
    cum = _dot(tri, onehot.astype(BF16)) + carry_s[...]
    rank1 = jnp.sum(jnp.where(sel1, cum, 0.0), axis=-1, keepdims=True)
    rank2 = jnp.sum(jnp.where(sel2, cum, 0.0), axis=-1, keepdims=True)
    rec = jnp.zeros(logits.shape, F32)
    for ln, val in ((R_E1, i1.astype(F32)), (R_E2, i2.astype(F32)), (R_RANK1, rank1), (R_RANK2, rank2),
                    (R_G1, g1), (R_G2, g2)):
        rec = jnp.where(lane == ln, val, rec)
    route_ref[...] = rec
    carry = carry_s[...] + jnp.sum(onehot, axis=0, keepdims=True)
    carry_s[...] = carry
    cnt_ref[...] = jnp.broadcast_to(carry, cnt_ref.shape)


def _route(x, a, b, mod, n2_g, wo_a, wo_b, rw_hi, rw_lo, seq, ctx):
    n = x.shape[0]
    return pl.pallas_call(
        _route_kernel,
        grid=(n // TM,),
        in_specs=[_row_spec(D_MODEL), _row_spec(512), _row_spec(512),
                  pl.BlockSpec((1, 8, D_MODEL), _mod_index(seq, ctx)),
                  _full_spec((1, D_MODEL)),
                  _full_spec((512, D_MODEL)),
                  _full_spec((512, D_MODEL)),
                  _full_spec((D_MODEL, LANES)),
                  _full_spec((D_MODEL, LANES))],
        out_specs=[_row_spec(D_MODEL), _row_spec(D_MODEL), _row_spec(LANES), _full_spec((8, LANES))],
        out_shape=[jax.ShapeDtypeStruct((n, D_MODEL), F32),
                   jax.ShapeDtypeStruct((n, D_MODEL), F32),
                   jax.ShapeDtypeStruct((n, LANES), F32),
                   jax.ShapeDtypeStruct((8, LANES), F32)],
        scratch_shapes=[pltpu.VMEM((1, LANES), F32)],
        compiler_params=_params(("arbitrary",)),
        name="route",
    )(x, a, b, mod, n2_g, wo_a, wo_b, rw_hi, rw_lo)


def _row_copy(src, src_row, dst, dst_row, sem):
    return pltpu.make_async_copy(src.at[pl.ds(src_row, 1)], dst.at[pl.ds(dst_row, 1)], sem)


def _scatter_kernel(pos_ref, h_ref, xs_in_ref, xs_ref, sem, *, ts):
    del xs_in_ref
    base = pl.program_id(0) * ts

    def issue(t, carry):
        row = base + t
        for k in range(2):
            _row_copy(h_ref, row, xs_ref, pos_ref[2 * row + k], sem).start()
        return carry

    lax.fori_loop(0, ts, issue, 0, unroll=8)

    def drain(t, carry):
        for k in range(2):
            _row_copy(h_ref, 0, xs_ref, 0, sem).wait()
        return carry

    lax.fori_loop(0, ts, drain, 0, unroll=8)


def _scatter_rows(pos, h2, xs_zero, ts):
    n = h2.shape[0]
    return pl.pallas_call(
        functools.partial(_scatter_kernel, ts=ts),
        grid_spec=pltpu.PrefetchScalarGridSpec(
            num_scalar_prefetch=1,
            grid=(n // ts,),
            in_specs=[pl.BlockSpec(memory_space=pl.ANY), pl.BlockSpec(memory_space=pl.ANY)],
            out_specs=pl.BlockSpec(memory_space=pl.ANY),
            scratch_shapes=[pltpu.SemaphoreType.DMA(())]),
        out_shape=jax.ShapeDtypeStruct(xs_zero.shape, xs_zero.dtype),
        input_output_aliases={2: 0},
        compiler_params=_params(("arbitrary",)),
        name="moe_scatter",
    )(pos, h2, xs_zero)


def _experts_kernel(te_ref, tv_ref, x_ref, w1_ref, w3_ref, w2_ref, o_ref):
    j = pl.program_id(0)

    @pl.when(tv_ref[j] == 1)
    def _():
        x = x_ref[...].astype(BF16)
        act = _silu(_dot(x, w1_ref[0])) * _dot(x, w3_ref[0])
        o_ref[...] = _dot(act.astype(BF16), w2_ref[0])

    @pl.when(tv_ref[j] == 0)
    def _():
        o_ref[...] = jnp.zeros_like(o_ref)


def _experts(tile_expert, tile_valid, xs, w1, w3, w2, tr):
    rows = xs.shape[0]
    f = w1.shape[2]
    return pl.pallas_call(
        _experts_kernel,
        grid_spec=pltpu.PrefetchScalarGridSpec(
            num_scalar_prefetch=2,
            grid=(rows // tr,),
            in_specs=[pl.BlockSpec((tr, D_MODEL), lambda j, te, tv: (j, 0)),
                      pl.BlockSpec((1, D_MODEL, f), lambda j, te, tv: (te[j], 0, 0)),
                      pl.BlockSpec((1, D_MODEL, f), lambda j, te, tv: (te[j], 0, 0)),
                      pl.BlockSpec((1, f, D_MODEL), lambda j, te, tv: (te[j], 0, 0))],
            out_specs=pl.BlockSpec((tr, D_MODEL), lambda j, te, tv: (j, 0))),
        out_shape=jax.ShapeDtypeStruct((rows, D_MODEL), F32),
        compiler_params=_params(("arbitrary",)),
        name="moe_experts",
    )(tile_expert, tile_valid, xs, w1, w3, w2)


def _combine_kernel(pos_ref, os_ref, x1_ref, route_ref, mod_ref, y_ref, buf, sem, *, tc):
    base = pl.program_id(0) * tc

    def issue(t, carry):
        row = base + t
        for k in range(2):
            _row_copy(os_ref, pos_ref[2 * row + k], buf.at[k], t, sem).start()
        return carry

    lax.fori_loop(0, tc, issue, 0, unroll=8)

    def drain(t, carry):
        for k in range(2):
            _row_copy(os_ref, 0, buf.at[k], 0, sem).wait()
        return carry

    lax.fori_loop(0, tc, drain, 0, unroll=8)
    rec = route_ref[...]
    g1 = rec[:, R_G1:R_G1 + 1]
    g2 = rec[:, R_G2:R_G2 + 1]
    y_ref[...] = x1_ref[...] + mod_ref[0][5:6] * (g1 * buf[0] + g2 * buf[1])


def _combine(pos, outs, x1, route, mod, seq, ctx, tc):
    n = x1.shape[0]
    per = seq // tc
    mod_idx = (lambda i, p: (8, 0, 0)) if ctx else (lambda i, p: (i // per, 0, 0))
    return pl.pallas_call(
        functools.partial(_combine_kernel, tc=tc),
        grid_spec=pltpu.PrefetchScalarGridSpec(
            num_scalar_prefetch=1,
            grid=(n // tc,),
            in_specs=[pl.BlockSpec(memory_space=pl.ANY),
                      pl.BlockSpec((tc, D_MODEL), lambda i, p: (i, 0)),
                      pl.BlockSpec((tc, LANES), lambda i, p: (i, 0)),
                      pl.BlockSpec((1, 8, D_MODEL), mod_idx)],
            out_specs=pl.BlockSpec((tc, D_MODEL), lambda i, p: (i, 0)),
            scratch_shapes=[pltpu.VMEM((2, tc, D_MODEL), F32), pltpu.SemaphoreType.DMA(())]),
        out_shape=jax.ShapeDtypeStruct((n, D_MODEL), F32),
        compiler_params=_params(("arbitrary",)),
        name="moe_combine",
    )(pos, outs, x1, route, mod)


def _moe(x, a, b, mod, n2_g, wo_a, wo_b, rw_hi, rw_lo, w1, w3, w2, seq, ctx, tr):
    n = x.shape[0]
    x1, h2, route, cnt = _route(x, a, b, mod, n2_g, wo_a, wo_b, rw_hi, rw_lo, seq, ctx)
    counts = cnt[0, :N_EXPERTS].astype(jnp.int32)
    tiles = (counts + (tr - 1)) // tr
    tile_end = jnp.cumsum(tiles)
    start = (tile_end - tiles) * tr
    n_tiles = 2 * n // tr + N_EXPERTS
    e12 = route[:, R_E1:R_E2 + 1].astype(jnp.int32)
    rank = route[:, R_RANK1:R_RANK2 + 1].astype(jnp.int32)
    pos = (start[e12] + rank).reshape(-1)
    jt = jnp.arange(n_tiles, dtype=jnp.int32)
    tile_valid = (jt < tile_end[-1]).astype(jnp.int32)
    tile_expert = jnp.searchsorted(tile_end, jnp.minimum(jt, tile_end[-1] - 1), side="right").astype(jnp.int32)
    xs = _scatter_rows(pos, h2, jnp.zeros((n_tiles * tr, D_MODEL), F32), ts=512)
    outs = _experts(tile_expert, tile_valid, xs, w1, w3, w2, tr)
    return _combine(pos, outs, x1, route, mod, seq, ctx, tc=256)


def _proj1_kernel(*refs, rope):
    if rope:
        (x_ref, mod_ref, g_ref, w_ref, dqg_ref, dkg_ref, qag_ref, wqb_ref, mqg_ref, kvg_ref,
         c_ref, s_ref, cm_ref, sm_ref, dq_ref, dk_ref, dv_ref, qm_ref, ckv_ref, kr_ref) = refs
    else:
        (x_ref, mod_ref, g_ref, w_ref, dqg_ref, dkg_ref, qag_ref, wqb_ref, mqg_ref, kvg_ref,
         dq_ref, dk_ref, dv_ref, qm_ref, ckv_ref, kr_ref) = refs
    mod = mod_ref[0]
    h = _mod_norm(x_ref[...], g_ref[...], mod[1:2], mod[0:1])
    z = _dot(h.astype(BF16), w_ref[...])
    first = _first64((TM, LANES))
    nb = DIFF_W // LANES
    dqs = []
    dks = []
    for b in range(nb):
        xq = _norm64(z[:, b * LANES:(b + 1) * LANES], dqg_ref[...])
        xk = _norm64(z[:, DIFF_W + b * LANES: DIFF_W + (b + 1) * LANES], dkg_ref[...])
        if rope:
            xq = _rope_blk(xq, c_ref[...], s_ref[...], DIFF_HD // 2, first)
            xk = _rope_blk(xk, c_ref[...], s_ref[...], DIFF_HD // 2, first)
        dqs.append((xq * (DIFF_HD ** -0.5)).astype(BF16))
        dks.append(xk)
    dq_ref[...] = jnp.concatenate(dqs, axis=1)
    dk_ref[...] = jnp.concatenate(dks, axis=1)
    dv_ref[...] = z[:, 2 * DIFF_W:3 * DIFF_W]
    base = 3 * DIFF_W
    qa = z[:, base:base + MLA_Q_RANK]
    qa = qa * lax.rsqrt(jnp.mean(qa * qa, axis=-1, keepdims=True) + RMS_EPS) * qag_ref[...]
    qm = _dot(qa.astype(BF16), wqb_ref[...])
    first_m = _first_mla((TM, LANES))
    qms = []
    for hh in range(MLA_HEADS):
        xb = _norm_pad(qm[:, hh * MLA_PAD:(hh + 1) * MLA_PAD], mqg_ref[...], MLA_QK)
        if rope:
            xb = _rope_blk(xb, cm_ref[...], sm_ref[...], MLA_ROPE // 2, first_m)
        qms.append((xb * (MLA_QK ** -0.5)).astype(BF16))
    qm_ref[...] = jnp.concatenate(qms, axis=1)
    base += MLA_Q_RANK
    kva = z[:, base:base + MLA_KV_RANK]
    ckv_ref[...] = kva * lax.rsqrt(jnp.mean(kva * kva, axis=-1, keepdims=True) + RMS_EPS) * kvg_ref[...]
    base += MLA_KV_RANK
    kr_ref[...] = z[:, base:base + MLA_ROPE]


def _proj1(x, mod, n1_g, w_in, dqg, dkg, qag, wqb, mqg, kvg, rope_tabs, seq, ctx):
    n = x.shape[0]
    rope = rope_tabs is not None
    in_specs = [_row_spec(D_MODEL),
                pl.BlockSpec((1, 8, D_MODEL), _mod_index(seq, ctx)),
                _full_spec((1, D_MODEL)),
                _full_spec((D_MODEL, ODD_IN_PAD)),
                _full_spec((1, LANES)),
                _full_spec((1, LANES)),
                _full_spec((1, MLA_Q_RANK)),
                _full_spec((MLA_Q_RANK, MLA_HEADS * MLA_PAD)),
                _full_spec((1, LANES)),
                _full_spec((1, MLA_KV_RANK))]
    args = [x, mod, n1_g, w_in, dqg, dkg, qag, wqb, mqg, kvg]
    if rope:
        in_specs += [_rope_spec(seq)] * 4
        args += list(rope_tabs)
    return pl.pallas_call(
        functools.partial(_proj1_kernel, rope=rope),
        grid=(n // TM,),
        in_specs=in_specs,
        out_specs=[_row_spec(DIFF_W), _row_spec(DIFF_W), _row_spec(DIFF_W),
                   _row_spec(MLA_HEADS * MLA_PAD), _row_spec(MLA_KV_RANK), _row_spec(MLA_ROPE)],
        out_shape=[jax.ShapeDtypeStruct((n, DIFF_W), BF16),
                   jax.ShapeDtypeStruct((n, DIFF_W), F32),
                   jax.ShapeDtypeStruct((n, DIFF_W), F32),
                   jax.ShapeDtypeStruct((n, MLA_HEADS * MLA_PAD), BF16),
                   jax.ShapeDtypeStruct((n, MLA_KV_RANK), F32),
                   jax.ShapeDtypeStruct((n, MLA_ROPE), F32)],
        compiler_params=_params(("parallel",)),
        name="proj1",
    )(*args)


def _mla_keys_kernel(*refs, rope, tm):
    if rope:
        ckv_ref, kr_ref, wk_ref, wv_ref, e_ref, g_ref, cm_ref, sm_ref, k_ref, v_ref = refs
    else:
        ckv_ref, kr_ref, wk_ref, wv_ref, e_ref, g_ref, k_ref, v_ref = refs
    ckv = ckv_ref[...].astype(BF16)
    kr = kr_ref[...]
    kr_hi = kr.astype(BF16)
    kr_lo = (kr - kr_hi.astype(F32)).astype(BF16)
    k = _dot(ckv, wk_ref[...]) + (_dot(kr_hi, e_ref[...]) + _dot(kr_lo, e_ref[...]))
    first_m = _first_mla((tm, LANES))
    ks = []
    for hh in range(MLA_HEADS):
        xb = _norm_pad(k[:, hh * MLA_PAD:(hh + 1) * MLA_PAD], g_ref[...], MLA_QK)
        if rope:
            xb = _rope_blk(xb, cm_ref[...], sm_ref[...], MLA_ROPE // 2, first_m)
        ks.append(xb.astype(BF16))
    k_ref[...] = jnp.concatenate(ks, axis=1)
    v_ref[...] = _dot(ckv, wv_ref[...]).astype(BF16)


def _mla_keys(ckv, kr, wk, wv, e_mat, mkg, rope_tabs, seq, tm):
    n = ckv.shape[0]
    rope = rope_tabs is not None
    row = lambda w: pl.BlockSpec((tm, w), lambda i: (i, 0))
    in_specs = [row(MLA_KV_RANK), row(MLA_ROPE),
                _full_spec((MLA_KV_RANK, MLA_HEADS * MLA_PAD)),
                _full_spec((MLA_KV_RANK, MLA_HEADS * MLA_V)),
                _full_spec((MLA_ROPE, MLA_HEADS * MLA_PAD)),
                _full_spec((1, LANES))]
    args = [ckv, kr, wk, wv, e_mat, mkg]
    if rope:
        per = seq // tm
        in_specs += [pl.BlockSpec((tm, LANES), lambda i: (i % per, 0))] * 2
        args += list(rope_tabs)
    return pl.pallas_call(
        functools.partial(_mla_keys_kernel, rope=rope, tm=tm),
        grid=(n // tm,),
        in_specs=in_specs,
        out_specs=[row(MLA_HEADS * MLA_PAD), row(MLA_HEADS * MLA_V)],
        out_shape=[jax.ShapeDtypeStruct((n, MLA_HEADS * MLA_PAD), BF16),
                   jax.ShapeDtypeStruct((n, MLA_HEADS * MLA_V), BF16)],
        compiler_params=_params(("parallel",)),
        name="mla_keys",
    )(*args)


def _axial_tables(n, rot_dim):
    rows = n // GRID_W
    row = jnp.repeat(jnp.arange(rows, dtype=jnp.int32), GRID_W).astype(F32)
    col = jnp.tile(jnp.arange(GRID_W, dtype=jnp.int32), rows).astype(F32)
    nf = rot_dim // 4
    inv = ROPE_THETA ** (-jnp.arange(nf, dtype=F32) / nf)
    ang = jnp.concatenate([row[:, None] * inv, col[:, None] * inv], axis=-1)
    return jnp.cos(ang), jnp.sin(ang)


def _rope_tabs64(n):
    cos, sin = _axial_tables(n, HEAD_DIM)
    return (jnp.concatenate([cos, cos, cos, cos], axis=-1),
            jnp.concatenate([-sin, sin, -sin, sin], axis=-1))


def _rope_tabs_mla(n):
    cos, sin = _axial_tables(n, MLA_ROPE)
    one = jnp.ones((n, MLA_NOPE), F32)
    zero = jnp.zeros((n, MLA_NOPE), F32)
    pad = jnp.zeros((n, MLA_PAD - MLA_QK), F32)
    return (jnp.concatenate([one, cos, cos, pad], axis=-1),
            jnp.concatenate([zero, -sin, sin, pad], axis=-1))


def _tile_gain(g, reps):
    return jnp.tile(g, reps).reshape(1, -1)


def _pad_gain_mla(g):
    return jnp.concatenate([g, jnp.zeros((MLA_PAD - MLA_QK,), F32)]).reshape(1, -1)


def kernel(x_prompt, x_sample, cache_l0_gqa_k, cache_l0_gqa_v, cache_l1_diff_k, cache_l1_diff_v, cache_l1_mla_ckv, cache_l1_mla_krope, c, c_ctx, l0_ada_w, l0_ada_b, l0_norm1_g, l0_norm2_g, l0_w_in, l0_conv_dw_w, l0_conv_dw_b, l0_conv_ln_g, l0_conv_ln_b, l0_q_norm_g, l0_k_norm_g, l0_w_out, l0_ffn_w1, l0_ffn_w3, l0_ffn_w2, l1_ada_w, l1_ada_b, l1_norm1_g, l1_norm2_g, l1_w_in, l1_diff_q_norm_g, l1_diff_k_norm_g, l1_lambda_q1, l1_lambda_k1, l1_lambda_q2, l1_lambda_k2, l1_diff_subln_g, l1_mla_q_a_norm_g, l1_mla_w_qb, l1_mla_kv_a_norm_g, l1_mla_w_kvb, l1_mla_q_norm_g, l1_mla_k_norm_g, l1_w_out, l1_router_w, l1_moe_w1, l1_moe_w3, l1_moe_w2):
    pb, ps, _ = x_prompt.shape
    sb, ss, _ = x_sample.shape
    past = cache_l0_gqa_k.shape[1]
    n_p = pb * ps
    n_s = sb * ss

    cond16 = jnp.concatenate([c, c_ctx[None, :], jnp.zeros((16 - sb - 1, D_MODEL), F32)], axis=0)

    def mod_table(ada_w, ada_b):
        m = _ada(cond16, ada_w, ada_b).reshape(16, 6, D_MODEL)
        return jnp.pad(m, ((0, 0), (0, 2), (0, 0)))

    mod0 = mod_table(l0_ada_w, l0_ada_b)
    mod1 = mod_table(l1_ada_w, l1_ada_b)

    tabs64 = _rope_tabs64(ss)
    tabs_mla = _rope_tabs_mla(ss)

    xp = x_prompt.reshape(n_p, D_MODEL)
    xs = x_sample.reshape(n_s, D_MODEL)

    w_in0 = l0_w_in.astype(BF16)
    qg0 = _tile_gain(l0_q_norm_g, 2)
    kg0 = _tile_gain(l0_k_norm_g, 2)
    n1g0 = l0_norm1_g.reshape(1, -1)
    n2g0 = l0_norm2_g.reshape(1, -1)
    wo0 = l0_w_out.astype(BF16)
    nc0 = D_FF // D_FF_EXPERT
    w1_0 = l0_ffn_w1.astype(BF16).reshape(D_MODEL, nc0, D_FF_EXPERT).transpose(1, 0, 2)
    w3_0 = l0_ffn_w3.astype(BF16).reshape(D_MODEL, nc0, D_FF_EXPERT).transpose(1, 0, 2)
    w2_0 = l0_ffn_w2.astype(BF16).reshape(nc0, D_FF_EXPERT, D_MODEL)

    def layer0(x, seq, bsz, ctx, tabs, cache):
        u, q, k, v = _proj0(x, mod0, n1g0, w_in0, qg0, kg0, tabs, seq, ctx)
        cv = _conv(u.reshape(bsz, seq, 2 * CONV_CH), l0_conv_dw_w, l0_conv_dw_b, l0_conv_ln_g, l0_conv_ln_b,
                   ts=min(seq, 512))
        k3 = k.reshape(bsz, seq, LANES)
        v3 = v.reshape(bsz, seq, LANES)
        segs = [(k3, v3)]
        if cache is not None:
            segs.append(cache)
        at = _gqa_attn(q.reshape(bsz, seq, 512), segs, tq=256)
        y = _post(x, cv.reshape(-1, CONV_CH), at.reshape(-1, 512), mod0, n2g0, wo0[:CONV_CH], wo0[CONV_CH:],
                  w1_0, w3_0, w2_0, seq, ctx)
        return y, k, v

    yp0, k0, v0 = layer0(xp, ps, pb, True, None, None)
    cache0 = (cache_l0_gqa_k.reshape(sb, past, LANES), cache_l0_gqa_v.reshape(sb, past, LANES))
    ys0, _, _ = layer0(xs, ss, sb, False, tabs64, cache0)

    w_in1 = jnp.pad(l1_w_in, ((0, 0), (0, ODD_IN_PAD - l1_w_in.shape[1]))).astype(BF16)
    dqg = _tile_gain(l1_diff_q_norm_g, 2)
    dkg = _tile_gain(l1_diff_k_norm_g, 2)
    qag = l1_mla_q_a_norm_g.reshape(1, -1)
    kvg = l1_mla_kv_a_norm_g.reshape(1, -1)
    wqb = jnp.pad(l1_mla_w_qb.reshape(MLA_Q_RANK, MLA_HEADS, MLA_QK),
                  ((0, 0), (0, 0), (0, MLA_PAD - MLA_QK))).reshape(MLA_Q_RANK, -1).astype(BF16)
    mqg = _pad_gain_mla(l1_mla_q_norm_g)
    mkg = _pad_gain_mla(l1_mla_k_norm_g)
    wkvb = l1_mla_w_kvb.reshape(MLA_KV_RANK, MLA_HEADS, MLA_NOPE + MLA_V)
    wk = jnp.pad(wkvb[:, :, :MLA_NOPE], ((0, 0), (0, 0), (0, MLA_PAD - MLA_NOPE))).reshape(MLA_KV_RANK, -1)
    wk = wk.astype(BF16)
    wv = wkvb[:, :, MLA_NOPE:].reshape(MLA_KV_RANK, -1).astype(BF16)
    e_one = jnp.concatenate([jnp.zeros((MLA_ROPE, MLA_NOPE), F32), jnp.eye(MLA_ROPE, dtype=F32),
                             jnp.zeros((MLA_ROPE, MLA_PAD - MLA_QK), F32)], axis=1)
    e_mat = jnp.tile(e_one, (1, MLA_HEADS)).astype(BF16)
    n1g1 = l1_norm1_g.reshape(1, -1)
    n2g1 = l1_norm2_g.reshape(1, -1)
    wo1 = l1_w_out.astype(BF16)
    rw = jnp.pad(l1_router_w, ((0, 0), (0, LANES - N_EXPERTS)))
    rw_hi = rw.astype(BF16)
    rw_lo = (rw - rw_hi.astype(F32)).astype(BF16)
    mw1 = l1_moe_w1.astype(BF16)
    mw3 = l1_moe_w3.astype(BF16)
    mw2 = l1_moe_w2.astype(BF16)
    lams = (l1_lambda_q1, l1_lambda_k1, l1_lambda_q2, l1_lambda_k2)

    def layer1(x, seq, bsz, ctx, tabs, caches):
        rope_tabs = None if tabs is None else (tabs[0][0], tabs[0][1], tabs[1][0], tabs[1][1])
        dq, dk, dv, qm, ckv, kr = _proj1(x, mod1, n1g1, w_in1, dqg, dkg, qag, wqb, mqg, kvg, rope_tabs, seq, ctx)
        km, vm = _mla_keys(ckv, kr, wk, wv, e_mat, mkg, None if tabs is None else tabs[1], seq, TM)
        dsegs = [(dk.reshape(bsz, seq, DIFF_W), dv.reshape(bsz, seq, DIFF_W))]
        msegs = [(km.reshape(bsz, seq, -1), vm.reshape(bsz, seq, -1))]
        if caches is not None:
            c_dk, c_dv, c_ckv, c_kr = caches
            dsegs.append((c_dk.reshape(sb, past, DIFF_W), c_dv.reshape(sb, past, DIFF_W)))
            ckm, cvm = _mla_keys(c_ckv.reshape(-1, MLA_KV_RANK), c_kr.reshape(-1, MLA_ROPE), wk, wv, e_mat, mkg,
                                 None, past, past)
            msegs.append((ckm.reshape(sb, past, -1), cvm.reshape(sb, past, -1)))
        od = _diff_attn(dq.reshape(bsz, seq, DIFF_W), dsegs, lams, l1_diff_subln_g, tq=256)
        om = _mla_attn(qm.reshape(bsz, seq, -1), msegs, tq=256)
        y = _moe(x, od.reshape(-1, DIFF_W), om.reshape(-1, 512), mod1, n2g1, wo1[:DIFF_W], wo1[DIFF_W:],
                 rw_hi, rw_lo, mw1, mw3, mw2, seq, ctx, tr=256 if ctx else 512)
        return y, dk, dv, ckv, kr

    yp1, dk1, dv1, ckv1, kr1 = layer1(yp0, ps, pb, True, None, None)
    ys1, _, _, _, _ = layer1(ys0, ss, sb, False, (tabs64, tabs_mla),
                             (cache_l1_diff_k, cache_l1_diff_v, cache_l1_mla_ckv, cache_l1_mla_krope))

    return (yp1.reshape(pb, ps, D_MODEL), ys1.reshape(sb, ss, D_MODEL),
            k0.reshape(pb, ps, GQA_KV_HEADS, HEAD_DIM), v0.reshape(pb, ps, GQA_KV_HEADS, HEAD_DIM),
            dk1.reshape(pb, ps, DIFF_HEADS, 2, DIFF_HD), dv1.reshape(pb, ps, DIFF_HEADS, 2 * DIFF_HD),
            ckv1.reshape(pb, ps, MLA_KV_RANK), kr1.reshape(pb, ps, MLA_ROPE))
```

```python
import functools
import math

import jax
import jax.numpy as jnp
from jax import lax
from jax.experimental import pallas as pl
from jax.experimental.pallas import tpu as pltpu

F32 = jnp.float32
BF16 = jnp.bfloat16

D_MODEL = 1024
GRID_W = 64
ROPE_THETA = 10000.0
RMS_EPS = 1e-6
LN_EPS = 1e-5

CONV_CH = 512
CONV_WIDTH = 31
CONV_PAD = 15
CONV_HALO = 16
HEAD_DIM = 64
GQA_HEADS = 8
GQA_KV_HEADS = 2
GQA_GROUP = 4
EVEN_IN = 1792

DIFF_HD = 64
DIFF_HEADS = 4
DIFF_W = 512
LAMBDA_INIT = 0.8 - 0.6 * math.exp(-0.3 * 1)
MLA_HEADS = 8
MLA_NOPE = 64
MLA_ROPE = 32
MLA_QK = 96
MLA_V = 64
MLA_Q_RANK = 256
MLA_KV_RANK = 128
MLA_PAD = 128
ODD_IN_PAD = 2048

D_FF = 2816
N_EXPERTS = 8
D_FF_EXPERT = 1408

LANES = 128
VMEM_LIMIT = 56 * 1024 * 1024

TM = 512
CONV_RC = 64


def _params(sem):
    return pltpu.CompilerParams(dimension_semantics=sem, vmem_limit_bytes=VMEM_LIMIT)


def _sigmoid(x):
    return 1.0 / (1.0 + jnp.exp(-x))


def _silu(x):
    return x * _sigmoid(x)


def _dot(a, b):
    return jnp.dot(a, b, preferred_element_type=F32)


def _dot_nt(a, b):
    return lax.dot_general(a, b, (((1,), (1,)), ((), ())), preferred_element_type=F32)


def _mod_norm(x, g, sc, sh):
    y = x * lax.rsqrt(jnp.mean(x * x, axis=-1, keepdims=True) + RMS_EPS)
    return (y * g) * (1.0 + sc) + sh


def _lane(shape):
    return lax.broadcasted_iota(jnp.int32, shape, len(shape) - 1)


def _norm64(xb, gain):
    lo = _lane(xb.shape) < 64
    sq = xb * xb
    s_lo = jnp.sum(jnp.where(lo, sq, 0.0), axis=-1, keepdims=True)
    s_hi = jnp.sum(jnp.where(lo, 0.0, sq), axis=-1, keepdims=True)
    ms = jnp.where(lo, s_lo, s_hi) * (1.0 / 64.0)
    return xb * lax.rsqrt(ms + RMS_EPS) * gain


def _norm_pad(xb, gain, n_real):
    ms = jnp.sum(xb * xb, axis=-1, keepdims=True) * (1.0 / n_real)
    return xb * lax.rsqrt(ms + RMS_EPS) * gain


def _rope_blk(xb, cos_t, sin_t, half, first):
    partner = jnp.where(first, pltpu.roll(xb, LANES - half, 1), pltpu.roll(xb, half, 1))
    return xb * cos_t + partner * sin_t


def _first64(shape):
    return (_lane(shape) % 64) < 32


def _first_mla(shape):
    lane = _lane(shape)
    return (lane >= MLA_NOPE) & (lane < MLA_NOPE + MLA_ROPE // 2)


def _ada_kernel(c_ref, w_ref, b_ref, o_ref):
    s = _silu(c_ref[...])
    o_ref[...] = _dot(s.astype(BF16), w_ref[...].astype(BF16)) + b_ref[...]


def _ada(cond16, ada_w, ada_b):
    tn = 1536
    return pl.pallas_call(
        _ada_kernel,
        grid=(6 * D_MODEL // tn,),
        in_specs=[pl.BlockSpec((16, D_MODEL), lambda j: (0, 0)),
                  pl.BlockSpec((D_MODEL, tn), lambda j: (0, j)),
                  pl.BlockSpec((1, tn), lambda j: (0, j))],
        out_specs=pl.BlockSpec((16, tn), lambda j: (0, j)),
        out_shape=jax.ShapeDtypeStruct((16, 6 * D_MODEL), F32),
        compiler_params=_params(("parallel",)),
        name="ada",
    )(cond16, ada_w, ada_b.reshape(1, -1))


def _mod_index(seq, ctx):
    if ctx:
        return lambda i: (8, 0, 0)
    per = seq // TM
    return lambda i: (i // per, 0, 0)


def _row_spec(width):
    return pl.BlockSpec((TM, width), lambda i: (i, 0))


def _full_spec(shape):
    nd = len(shape)
    return pl.BlockSpec(shape, lambda *a: (0,) * nd)


def _rope_spec(seq):
    per = seq // TM
    return pl.BlockSpec((TM, LANES), lambda i: (i % per, 0))


def _proj0_kernel(*refs, rope):
    if rope:
        x_ref, mod_ref, g_ref, w_ref, qg_ref, kg_ref, c_ref, s_ref, u_ref, q_ref, k_ref, v_ref = refs
    else:
        x_ref, mod_ref, g_ref, w_ref, qg_ref, kg_ref, u_ref, q_ref, k_ref, v_ref = refs
    mod = mod_ref[0]
    h = _mod_norm(x_ref[...], g_ref[...], mod[1:2], mod[0:1])
    z = _dot(h.astype(BF16), w_ref[...])
    u_ref[...] = z[:, :2 * CONV_CH]
    base = 2 * CONV_CH
    first = _first64((TM, LANES))
    qs = []
    for b in range(GQA_HEADS * HEAD_DIM // LANES):
        xb = _norm64(z[:, base + b * LANES: base + (b + 1) * LANES], qg_ref[...])
        if rope:
            xb = _rope_blk(xb, c_ref[...], s_ref[...], HEAD_DIM // 2, first)
        qs.append((xb * (HEAD_DIM ** -0.5)).astype(BF16))
    q_ref[...] = jnp.concatenate(qs, axis=1)
    base += GQA_HEADS * HEAD_DIM
    kb = _norm64(z[:, base: base + LANES], kg_ref[...])
    if rope:
        kb = _rope_blk(kb, c_ref[...], s_ref[...], HEAD_DIM // 2, first)
    k_ref[...] = kb
    v_ref[...] = z[:, base + LANES: base + 2 * LANES]


def _proj0(x, mod, n1_g, w_in, qg, kg, rope_tabs, seq, ctx):
    n = x.shape[0]
    rope = rope_tabs is not None
    in_specs = [_row_spec(D_MODEL),
                pl.BlockSpec((1, 8, D_MODEL), _mod_index(seq, ctx)),
                _full_spec((1, D_MODEL)),
                _full_spec((D_MODEL, EVEN_IN)),
                _full_spec((1, LANES)),
                _full_spec((1, LANES))]
    args = [x, mod, n1_g, w_in, qg, kg]
    if rope:
        in_specs += [_rope_spec(seq), _rope_spec(seq)]
        args += list(rope_tabs)
    return pl.pallas_call(
        functools.partial(_proj0_kernel, rope=rope),
        grid=(n // TM,),
        in_specs=in_specs,
        out_specs=[_row_spec(2 * CONV_CH), _row_spec(512), _row_spec(LANES), _row_spec(LANES)],
        out_shape=[jax.ShapeDtypeStruct((n, 2 * CONV_CH), F32),
                   jax.ShapeDtypeStruct((n, 512), BF16),
                   jax.ShapeDtypeStruct((n, LANES), F32),
                   jax.ShapeDtypeStruct((n, LANES), F32)],
        compiler_params=_params(("parallel",)),
        name="proj0",
    )(*args)


def _conv_kernel(um_ref, up_ref, un_ref, w_ref, b_ref, lg_ref, lb_ref, o_ref, gp_ref, *, ts):
    i = pl.program_id(1)
    n_t = pl.num_programs(1)

    def glu(u):
        return u[:, :CONV_CH] * _sigmoid(u[:, CONV_CH:])

    gp_ref[CONV_HALO:CONV_HALO + ts, :] = glu(um_ref[0])
    gp_ref[0:CONV_HALO, :] = jnp.where(i > 0, glu(up_ref[0]), 0.0)
    gp_ref[CONV_HALO + ts:2 * CONV_HALO + ts, :] = jnp.where(i < n_t - 1, glu(un_ref[0]), 0.0)
    w = w_ref[...]
    off = CONV_HALO - CONV_PAD
    for c in range(ts // CONV_RC):
        acc = jnp.zeros((CONV_RC, CONV_CH), F32)
        for k in range(CONV_WIDTH):
            acc = acc + gp_ref[pl.ds(c * CONV_RC + k + off, CONV_RC), :] * w[k:k + 1, :]
        y = acc + b_ref[...]
        mu = jnp.mean(y, axis=-1, keepdims=True)
        yc = y - mu
        var = jnp.mean(yc * yc, axis=-1, keepdims=True)
        yn = yc * lax.rsqrt(var + LN_EPS) * lg_ref[...] + lb_ref[...]
        o_ref[0, c * CONV_RC:(c + 1) * CONV_RC, :] = _silu(yn).astype(BF16)


def _conv(u, dw_w, dw_b, ln_g, ln_b, ts):
    bsz, seq, _ = u.shape
    n_t = seq // ts
    hb = ts // CONV_HALO
    last_hb = seq // CONV_HALO - 1
    return pl.pallas_call(
        functools.partial(_conv_kernel, ts=ts),
        grid=(bsz, n_t),
        in_specs=[pl.BlockSpec((1, ts, 2 * CONV_CH), lambda b, i: (b, i, 0)),
                  pl.BlockSpec((1, CONV_HALO, 2 * CONV_CH), lambda b, i: (b, jnp.maximum(i * hb - 1, 0), 0)),
                  pl.BlockSpec((1, CONV_HALO, 2 * CONV_CH),
                               lambda b, i: (b, jnp.minimum((i + 1) * hb, last_hb), 0)),
                  _full_spec((CONV_WIDTH, CONV_CH)),
                  _full_spec((1, CONV_CH)),
                  _full_spec((1, CONV_CH)),
                  _full_spec((1, CONV_CH))],
        out_specs=pl.BlockSpec((1, ts, CONV_CH), lambda b, i: (b, i, 0)),
        out_shape=jax.ShapeDtypeStruct((bsz, seq, CONV_CH), BF16),
        scratch_shapes=[pltpu.VMEM((ts + 2 * CONV_HALO, CONV_CH), F32)],
        compiler_params=_params(("parallel", "parallel")),
        name="conv",
    )(u, u, u, dw_w, dw_b.reshape(1, -1), ln_g.reshape(1, -1), ln_b.reshape(1, -1))


def _softmax_pv(q, ks, vs):
    ss = [_dot_nt(q, k) for k in ks]
    m = ss[0].max(axis=-1, keepdims=True)
    for s in ss[1:]:
        m = jnp.maximum(m, s.max(axis=-1, keepdims=True))
    den = None
    o = None
    for s, v in zip(ss, vs):
        p = jnp.exp(s - m)
        d = jnp.sum(p, axis=-1, keepdims=True)
        t = _dot(p.astype(BF16), v)
        den = d if den is None else den + d
        o = t if o is None else o + t
    return o * (1.0 / den)


def _gqa_kernel(*refs, n_seg, tq):
    q_ref = refs[0]
    kv = refs[1:1 + 2 * n_seg]
    o_ref = refs[1 + 2 * n_seg]
    q = q_ref[0]
    outs = []
    for g in range(GQA_KV_HEADS):
        sl = slice(g * HEAD_DIM, (g + 1) * HEAD_DIM)
        qs = jnp.concatenate(
            [q[:, (GQA_GROUP * g + j) * HEAD_DIM:(GQA_GROUP * g + j + 1) * HEAD_DIM] for j in range(GQA_GROUP)],
            axis=0)
        ks = [kv[2 * s][0, :, sl].astype(BF16) for s in range(n_seg)]
        vs = [kv[2 * s + 1][0, :, sl].astype(BF16) for s in range(n_seg)]
        o = _softmax_pv(qs, ks, vs)
        for j in range(GQA_GROUP):
            outs.append(o[j * tq:(j + 1) * tq])
    o_ref[0] = jnp.concatenate(outs, axis=1).astype(BF16)


def _kv_specs(segs):
    specs = []
    for k, v in segs:
        specs.append(pl.BlockSpec((1,) + k.shape[1:], lambda b, i: (b, 0, 0)))
        specs.append(pl.BlockSpec((1,) + v.shape[1:], lambda b, i: (b, 0, 0)))
    return specs


def _gqa_attn(q, segs, tq):
    bsz, sq, _ = q.shape
    args = [q]
    for k, v in segs:
        args += [k, v]
    return pl.pallas_call(
        functools.partial(_gqa_kernel, n_seg=len(segs), tq=tq),
        grid=(bsz, sq // tq),
        in_specs=[pl.BlockSpec((1, tq, 512), lambda b, i: (b, i, 0))] + _kv_specs(segs),
        out_specs=pl.BlockSpec((1, tq, 512), lambda b, i: (b, i, 0)),
        out_shape=jax.ShapeDtypeStruct((bsz, sq, 512), BF16),
        compiler_params=_params(("parallel", "parallel")),
        name="gqa_attn",
    )(*args)


def _diff_kernel(*refs, n_seg):
    q_ref = refs[0]
    kv = refs[1:1 + 2 * n_seg]
    lq1, lk1, lq2, lk2, sg_ref, o_ref = refs[1 + 2 * n_seg:]
    lam = (jnp.exp(jnp.sum(lq1[...] * lk1[...], axis=-1, keepdims=True))
           - jnp.exp(jnp.sum(lq2[...] * lk2[...], axis=-1, keepdims=True)) + LAMBDA_INIT)
    q = q_ref[0]
    outs = []
    for h in range(DIFF_HEADS):
        vs = [kv[2 * s + 1][0, :, h * 2 * DIFF_HD:(h + 1) * 2 * DIFF_HD].astype(BF16) for s in range(n_seg)]
        o12 = []
        for j in range(2):
            sl = slice((2 * h + j) * DIFF_HD, (2 * h + j + 1) * DIFF_HD)
            ks = [kv[2 * s][0, :, sl].astype(BF16) for s in range(n_seg)]
            o12.append(_softmax_pv(q[:, sl], ks, vs))
        dlt = o12[0] - lam * o12[1]
        ms = jnp.mean(dlt * dlt, axis=-1, keepdims=True)
        od = dlt * lax.rsqrt(ms + RMS_EPS) * sg_ref[...] * (1.0 - LAMBDA_INIT)
        outs.append(od.astype(BF16))
    o_ref[0] = jnp.concatenate(outs, axis=1)


def _diff_attn(q, segs, lams, sub_g, tq):
    bsz, sq, _ = q.shape
    args = [q]
    for k, v in segs:
        args += [k, v]
    args += [l.reshape(1, -1) for l in lams] + [sub_g.reshape(1, -1)]
    return pl.pallas_call(
        functools.partial(_diff_kernel, n_seg=len(segs)),
        grid=(bsz, sq // tq),
        in_specs=([pl.BlockSpec((1, tq, DIFF_W), lambda b, i: (b, i, 0))] + _kv_specs(segs)
                  + [_full_spec((1, DIFF_HD))] * 4 + [_full_spec((1, 2 * DIFF_HD))]),
        out_specs=pl.BlockSpec((1, tq, DIFF_W), lambda b, i: (b, i, 0)),
        out_shape=jax.ShapeDtypeStruct((bsz, sq, DIFF_W), BF16),
        compiler_params=_params(("parallel", "parallel")),
        name="diff_attn",
    )(*args)


def _mla_kernel(*refs, n_seg):
    q_ref = refs[0]
    kv = refs[1:1 + 2 * n_seg]
    o_ref = refs[1 + 2 * n_seg]
    q = q_ref[0]
    outs = []
    for h in range(MLA_HEADS):
        ks = [kv[2 * s][0, :, h * MLA_PAD:(h + 1) * MLA_PAD] for s in range(n_seg)]
        vs = [kv[2 * s + 1][0, :, h * MLA_V:(h + 1) * MLA_V] for s in range(n_seg)]
        outs.append(_softmax_pv(q[:, h * MLA_PAD:(h + 1) * MLA_PAD], ks, vs))
    o_ref[0] = jnp.concatenate(outs, axis=1).astype(BF16)


def _mla_attn(q, segs, tq):
    bsz, sq, _ = q.shape
    args = [q]
    for k, v in segs:
        args += [k, v]
    return pl.pallas_call(
        functools.partial(_mla_kernel, n_seg=len(segs)),
        grid=(bsz, sq // tq),
        in_specs=[pl.BlockSpec((1, tq, MLA_HEADS * MLA_PAD), lambda b, i: (b, i, 0))] + _kv_specs(segs),
        out_specs=pl.BlockSpec((1, tq, MLA_HEADS * MLA_V), lambda b, i: (b, i, 0)),
        out_shape=jax.ShapeDtypeStruct((bsz, sq, MLA_HEADS * MLA_V), BF16),
        compiler_params=_params(("parallel", "parallel")),
        name="mla_attn",
    )(*args)


def _post_kernel(x_ref, a_ref, b_ref, mod_ref, g_ref, woa_ref, wob_ref,
                 w1_ref, w3_ref, w2_ref, y_ref, x1_s, h2_s, acc_s):
    j = pl.program_id(1)

    @pl.when(j == 0)
    def _():
        mod = mod_ref[0]
        o = _dot(a_ref[...], woa_ref[...]) + _dot(b_ref[...], wob_ref[...])
        x1 = x_ref[...] + mod[2:3] * o
        x1_s[...] = x1
        h2_s[...] = _mod_norm(x1, g_ref[...], mod[4:5], mod[3:4]).astype(BF16)
        acc_s[...] = jnp.zeros_like(acc_s)

    h2 = h2_s[...]
    act = _silu(_dot(h2, w1_ref[0])) * _dot(h2, w3_ref[0])
    acc_s[...] += _dot(act.astype(BF16), w2_ref[0])

    @pl.when(j == pl.num_programs(1) - 1)
    def _():
        y_ref[...] = x1_s[...] + mod_ref[0][5:6] * acc_s[...]


def _post(x, a, b, mod, n2_g, wo_a, wo_b, w1, w3, w2, seq, ctx):
    n = x.shape[0]
    nc, _, f = w1.shape
    mod_idx = _mod_index(seq, ctx)
    return pl.pallas_call(
        _post_kernel,
        grid=(n // TM, nc),
        in_specs=[pl.BlockSpec((TM, D_MODEL), lambda i, j: (i, 0)),
                  pl.BlockSpec((TM, 512), lambda i, j: (i, 0)),
                  pl.BlockSpec((TM, 512), lambda i, j: (i, 0)),
                  pl.BlockSpec((1, 8, D_MODEL), lambda i, j: mod_idx(i)),
                  _full_spec((1, D_MODEL)),
                  _full_spec((512, D_MODEL)),
                  _full_spec((512, D_MODEL)),
                  pl.BlockSpec((1, D_MODEL, f), lambda i, j: (j, 0, 0)),
                  pl.BlockSpec((1, D_MODEL, f), lambda i, j: (j, 0, 0)),
                  pl.BlockSpec((1, f, D_MODEL), lambda i, j: (j, 0, 0))],
        out_specs=pl.BlockSpec((TM, D_MODEL), lambda i, j: (i, 0)),
        out_shape=jax.ShapeDtypeStruct((n, D_MODEL), F32),
        scratch_shapes=[pltpu.VMEM((TM, D_MODEL), F32), pltpu.VMEM((TM, D_MODEL), BF16),
                        pltpu.VMEM((TM, D_MODEL), F32)],
        compiler_params=_params(("parallel", "arbitrary")),
        name="post_ffn",
    )(x, a, b, mod, n2_g, wo_a, wo_b, w1, w3, w2)


R_E1, R_E2, R_RANK1, R_RANK2, R_G1, R_G2 = range(6)


def _route_kernel(x_ref, a_ref, b_ref, mod_ref, g_ref, woa_ref, wob_ref, rhi_ref, rlo_ref,
                  x1_ref, h2_ref, route_ref, cnt_ref, carry_s):
    i = pl.program_id(0)

    @pl.when(i == 0)
    def _():
        carry_s[...] = jnp.zeros_like(carry_s)

    mod = mod_ref[0]
    o = _dot(a_ref[...], woa_ref[...]) + _dot(b_ref[...], wob_ref[...])
    x1 = x_ref[...] + mod[2:3] * o
    x1_ref[...] = x1
    h2 = _mod_norm(x1, g_ref[...], mod[4:5], mod[3:4])
    h2_ref[...] = h2
    hi = h2.astype(BF16)
    lo = (h2 - hi.astype(F32)).astype(BF16)
    logits = _dot(hi, rhi_ref[...]) + (_dot(lo, rhi_ref[...]) + _dot(hi, rlo_ref[...]))
    lane = _lane(logits.shape)
    logits = jnp.where(lane < N_EXPERTS, logits, -1e30)
    m1 = logits.max(axis=-1, keepdims=True)
    i1 = jnp.min(jnp.where(logits == m1, lane, LANES), axis=-1, keepdims=True)
    rest = jnp.where(lane == i1, -jnp.inf, logits)
    m2 = rest.max(axis=-1, keepdims=True)
    i2 = jnp.min(jnp.where(rest == m2, lane, LANES), axis=-1, keepdims=True)
    e = jnp.exp(m2 - m1)
    g1 = 1.0 / (1.0 + e)
    g2 = e * g1
    sel1 = lane == i1
    sel2 = lane == i2
    onehot = jnp.where(sel1, 1.0, 0.0) + jnp.where(sel2, 1.0, 0.0)
    rr = lax.broadcasted_iota(jnp.int32, (TM, TM), 0)
    cc = lax.broadcasted_iota(jnp.int32, (TM, TM), 1)
    tri = jnp.where(cc < rr, 1.0, 0.0).astype(BF16)
    cum = _dot(tri, onehot.astype(BF16)) + carry_s[...]
    rank1 = jnp.sum(jnp.where(sel1, cum, 0.0), axis=-1, keepdims=True)
    rank2 = jnp.sum(jnp.where(sel2, cum, 0.0), axis=-1, keepdims=True)
    rec = jnp.zeros(logits.shape, F32)
    for ln, val in ((R_E1, i1.astype(F32)), (R_E2, i2.astype(F32)), (R_RANK1, rank1), (R_RANK2, rank2),
                    (R_G1, g1), (R_G2, g2)):
        rec = jnp.where(lane == ln, val, rec)
    route_ref[...] = rec
    carry = carry_s[...] + jnp.sum(onehot, axis=0, keepdims=True)
    carry_s[...] = carry
    cnt_ref[...] = jnp.broadcast_to(carry, cnt_ref.shape)


def _route(x, a, b, mod, n2_g, wo_a, wo_b, rw_hi, rw_lo, seq, ctx):
    n = x.shape[0]
    return pl.pallas_call(
        _route_kernel,
        grid=(n // TM,),
        in_specs=[_row_spec(D_MODEL), _row_spec(512), _row_spec(512),
                  pl.BlockSpec((1, 8, D_MODEL), _mod_index(seq, ctx)),
                  _full_spec((1, D_MODEL)),
                  _full_spec((512, D_MODEL)),
                  _full_spec((512, D_MODEL)),
                  _full_spec((D_MODEL, LANES)),
                  _full_spec((D_MODEL, LANES))],
        out_specs=[_row_spec(D_MODEL), _row_spec(D_MODEL), _row_spec(LANES), _full_spec((8, LANES))],
        out_shape=[jax.ShapeDtypeStruct((n, D_MODEL), F32),
                   jax.ShapeDtypeStruct((n, D_MODEL), F32),
                   jax.ShapeDtypeStruct((n, LANES), F32),
                   jax.ShapeDtypeStruct((8, LANES), F32)],
        scratch_shapes=[pltpu.VMEM((1, LANES), F32)],
        compiler_params=_params(("arbitrary",)),
        name="route",
    )(x, a, b, mod, n2_g, wo_a, wo_b, rw_hi, rw_lo)


def _row_copy(src, src_row, dst, dst_row, sem):
    return pltpu.make_async_copy(src.at[pl.ds(src_row, 1)], dst.at[pl.ds(dst_row, 1)], sem)


def _scatter_kernel(pos_ref, h_ref, xs_in_ref, xs_ref, sem, *, ts):
    del xs_in_ref
    base = pl.program_id(0) * ts

    def issue(t, carry):
        for k in range(2):
            _row_copy(h_ref, t, xs_ref, pos_ref[2 * (base + t) + k], sem).start(priority=k)
        return carry

    lax.fori_loop(0, ts, issue, 0, unroll=8)

    def drain(t, carry):
        for k in range(2):
            _row_copy(h_ref, 0, xs_ref, 0, sem).wait()
        return carry

    lax.fori_loop(0, ts, drain, 0, unroll=8)


def _scatter_rows(pos, h2, xs_zero, ts):
    n = h2.shape[0]
    return pl.pallas_call(
        functools.partial(_scatter_kernel, ts=ts),
        grid_spec=pltpu.PrefetchScalarGridSpec(
            num_scalar_prefetch=1,
            grid=(n // ts,),
            in_specs=[pl.BlockSpec((ts, D_MODEL), lambda i, p: (i, 0)), pl.BlockSpec(memory_space=pl.ANY)],
            out_specs=pl.BlockSpec(memory_space=pl.ANY),
            scratch_shapes=[pltpu.SemaphoreType.DMA(())]),
        out_shape=jax.ShapeDtypeStruct(xs_zero.shape, xs_zero.dtype),
        input_output_aliases={2: 0},
        compiler_params=_params(("arbitrary",)),
        name="moe_scatter",
    )(pos, h2, xs_zero)


def _experts_kernel(te_ref, tv_ref, x_ref, w1_ref, w3_ref, w2_ref, o_ref):
    j = pl.program_id(0)

    @pl.when(tv_ref[j] == 1)
    def _():
        x = x_ref[...].astype(BF16)
        act = _silu(_dot(x, w1_ref[0])) * _dot(x, w3_ref[0])
        o_ref[...] = _dot(act.astype(BF16), w2_ref[0])

    @pl.when(tv_ref[j] == 0)
    def _():
        o_ref[...] = jnp.zeros_like(o_ref)


def _experts(tile_expert, tile_valid, xs, w1, w3, w2, tr):
    rows = xs.shape[0]
    f = w1.shape[2]
    return pl.pallas_call(
        _experts_kernel,
        grid_spec=pltpu.PrefetchScalarGridSpec(
            num_scalar_prefetch=2,
            grid=(rows // tr,),
            in_specs=[pl.BlockSpec((tr, D_MODEL), lambda j, te, tv: (j, 0)),
                      pl.BlockSpec((1, D_MODEL, f), lambda j, te, tv: (te[j], 0, 0)),
                      pl.BlockSpec((1, D_MODEL, f), lambda j, te, tv: (te[j], 0, 0)),
                      pl.BlockSpec((1, f, D_MODEL), lambda j, te, tv: (te[j], 0, 0))],
            out_specs=pl.BlockSpec((tr, D_MODEL), lambda j, te, tv: (j, 0))),
        out_shape=jax.ShapeDtypeStruct((rows, D_MODEL), F32),
        compiler_params=_params(("arbitrary",)),
        name="moe_experts",
    )(tile_expert, tile_valid, xs, w1, w3, w2)


def _combine_kernel(pos_ref, os_ref, x1_ref, route_ref, mod_ref, y_ref, buf, sem, *, tc):
    base = pl.program_id(0) * tc

    def issue(t, carry):
        row = base + t
        for k in range(2):
            _row_copy(os_ref, pos_ref[2 * row + k], buf.at[k], t, sem).start(priority=k)
        return carry

    lax.fori_loop(0, tc, issue, 0, unroll=8)

    def drain(t, carry):
        for k in range(2):
            _row_copy(os_ref, 0, buf.at[k], 0, sem).wait()
        return carry

    lax.fori_loop(0, tc, drain, 0, unroll=8)
    rec = route_ref[...]
    g1 = rec[:, R_G1:R_G1 + 1]
    g2 = rec[:, R_G2:R_G2 + 1]
    y_ref[...] = x1_ref[...] + mod_ref[0][5:6] * (g1 * buf[0] + g2 * buf[1])


def _combine(pos, outs, x1, route, mod, seq, ctx, tc):
    n = x1.shape[0]
    per = seq // tc
    mod_idx = (lambda i, p: (8, 0, 0)) if ctx else (lambda i, p: (i // per, 0, 0))
    return pl.pallas_call(
        functools.partial(_combine_kernel, tc=tc),
        grid_spec=pltpu.PrefetchScalarGridSpec(
            num_scalar_prefetch=1,
            grid=(n // tc,),
            in_specs=[pl.BlockSpec(memory_space=pl.ANY),
                      pl.BlockSpec((tc, D_MODEL), lambda i, p: (i, 0)),
                      pl.BlockSpec((tc, LANES), lambda i, p: (i, 0)),
                      pl.BlockSpec((1, 8, D_MODEL), mod_idx)],
            out_specs=pl.BlockSpec((tc, D_MODEL), lambda i, p: (i, 0)),
            scratch_shapes=[pltpu.VMEM((2, tc, D_MODEL), F32), pltpu.SemaphoreType.DMA(())]),
        out_shape=jax.ShapeDtypeStruct((n, D_MODEL), F32),
        compiler_params=_params(("arbitrary",)),
        name="moe_combine",
    )(pos, outs, x1, route, mod)


def _moe(x, a, b, mod, n2_g, wo_a, wo_b, rw_hi, rw_lo, w1, w3, w2, seq, ctx, tr):
    n = x.shape[0]
    x1, h2, route, cnt = _route(x, a, b, mod, n2_g, wo_a, wo_b, rw_hi, rw_lo, seq, ctx)
    counts = cnt[0, :N_EXPERTS].astype(jnp.int32)
    tiles = jnp.right_shift(counts + (tr - 1), int(math.log2(tr)))
    tile_end = jnp.cumsum(tiles)
    start = (tile_end - tiles) * tr
    n_tiles = 2 * n // tr + N_EXPERTS
    e12 = route[:, R_E1:R_E2 + 1].astype(jnp.int32)
    rank = route[:, R_RANK1:R_RANK2 + 1].astype(jnp.int32)
    experts = jnp.arange(N_EXPERTS, dtype=jnp.int32)
    start_of = jnp.sum(jnp.where(e12[:, :, None] == experts, start, 0), axis=-1)
    pos = (start_of + rank).reshape(-1)
    jt = jnp.arange(n_tiles, dtype=jnp.int32)
    tile_valid = (jt < tile_end[-1]).astype(jnp.int32)
    jc = jnp.minimum(jt, tile_end[-1] - 1)
    tile_expert = jnp.sum((jc[:, None] >= tile_end[None, :]).astype(jnp.int32), axis=1)
    xs = _scatter_rows(pos, h2, jnp.zeros((n_tiles * tr, D_MODEL), F32), ts=512)
    outs = _experts(tile_expert, tile_valid, xs, w1, w3, w2, tr)
    return _combine(pos, outs, x1, route, mod, seq, ctx, tc=256)


def _proj1_kernel(*refs, rope):
    if rope:
        (x_ref, mod_ref, g_ref, w_ref, dqg_ref, dkg_ref, qag_ref, wqb_ref, mqg_ref, kvg_ref,
         c_ref, s_ref, cm_ref, sm_ref, dq_ref, dk_ref, dv_ref, qm_ref, ckv_ref, kr_ref) = refs
    else:
        (x_ref, mod_ref, g_ref, w_ref, dqg_ref, dkg_ref, qag_ref, wqb_ref, mqg_ref, kvg_ref,
         dq_ref, dk_ref, dv_ref, qm_ref, ckv_ref, kr_ref) = refs
    mod = mod_ref[0]
    h = _mod_norm(x_ref[...], g_ref[...], mod[1:2], mod[0:1])
    z = _dot(h.astype(BF16), w_ref[...])
    first = _first64((TM, LANES))
    nb = DIFF_W // LANES
    dqs = []
    dks = []
    for b in range(nb):
        xq = _norm64(z[:, b * LANES:(b + 1) * LANES], dqg_ref[...])
        xk = _norm64(z[:, DIFF_W + b * LANES: DIFF_W + (b + 1) * LANES], dkg_ref[...])
        if rope:
            xq = _rope_blk(xq, c_ref[...], s_ref[...], DIFF_HD // 2, first)
            xk = _rope_blk(xk, c_ref[...], s_ref[...], DIFF_HD // 2, first)
        dqs.append((xq * (DIFF_HD ** -0.5)).astype(BF16))
        dks.append(xk)
    dq_ref[...] = jnp.concatenate(dqs, axis=1)
    dk_ref[...] = jnp.concatenate(dks, axis=1)
    dv_ref[...] = z[:, 2 * DIFF_W:3 * DIFF_W]
    base = 3 * DIFF_W
    qa = z[:, base:base + MLA_Q_RANK]
    qa = qa * lax.rsqrt(jnp.mean(qa * qa, axis=-1, keepdims=True) + RMS_EPS) * qag_ref[...]
    qm = _dot(qa.astype(BF16), wqb_ref[...])
    first_m = _first_mla((TM, LANES))
    qms = []
    for hh in range(MLA_HEADS):
        xb = _norm_pad(qm[:, hh * MLA_PAD:(hh + 1) * MLA_PAD], mqg_ref[...], MLA_QK)
        if rope:
            xb = _rope_blk(xb, cm_ref[...], sm_ref[...], MLA_ROPE // 2, first_m)
        qms.append((xb * (MLA_QK ** -0.5)).astype(BF16))
    qm_ref[...] = jnp.concatenate(qms, axis=1)
    base += MLA_Q_RANK
    kva = z[:, base:base + MLA_KV_RANK]
    ckv_ref[...] = kva * lax.rsqrt(jnp.mean(kva * kva, axis=-1, keepdims=True) + RMS_EPS) * kvg_ref[...]
    base += MLA_KV_RANK
    kr_ref[...] = z[:, base:base + MLA_ROPE]


def _proj1(x, mod, n1_g, w_in, dqg, dkg, qag, wqb, mqg, kvg, rope_tabs, seq, ctx):
    n = x.shape[0]
    rope = rope_tabs is not None
    in_specs = [_row_spec(D_MODEL),
                pl.BlockSpec((1, 8, D_MODEL), _mod_index(seq, ctx)),
                _full_spec((1, D_MODEL)),
                _full_spec((D_MODEL, ODD_IN_PAD)),
                _full_spec((1, LANES)),
                _full_spec((1, LANES)),
                _full_spec((1, MLA_Q_RANK)),
                _full_spec((MLA_Q_RANK, MLA_HEADS * MLA_PAD)),
                _full_spec((1, LANES)),
                _full_spec((1, MLA_KV_RANK))]
    args = [x, mod, n1_g, w_in, dqg, dkg, qag, wqb, mqg, kvg]
    if rope:
        in_specs += [_rope_spec(seq)] * 4
        args += list(rope_tabs)
    return pl.pallas_call(
        functools.partial(_proj1_kernel, rope=rope),
        grid=(n // TM,),
        in_specs=in_specs,
        out_specs=[_row_spec(DIFF_W), _row_spec(DIFF_W), _row_spec(DIFF_W),
                   _row_spec(MLA_HEADS * MLA_PAD), _row_spec(MLA_KV_RANK), _row_spec(MLA_ROPE)],
        out_shape=[jax.ShapeDtypeStruct((n, DIFF_W), BF16),
                   jax.ShapeDtypeStruct((n, DIFF_W), F32),
                   jax.ShapeDtypeStruct((n, DIFF_W), F32),
                   jax.ShapeDtypeStruct((n, MLA_HEADS * MLA_PAD), BF16),
                   jax.ShapeDtypeStruct((n, MLA_KV_RANK), F32),
                   jax.ShapeDtypeStruct((n, MLA_ROPE), F32)],
        compiler_params=_params(("parallel",)),
        name="proj1",
    )(*args)


def _mla_keys_kernel(*refs, rope, tm):
    if rope:
        ckv_ref, kr_ref, wk_ref, wv_ref, e_ref, g_ref, cm_ref, sm_ref, k_ref, v_ref = refs
    else:
        ckv_ref, kr_ref, wk_ref, wv_ref, e_ref, g_ref, k_ref, v_ref = refs
    ckv = ckv_ref[...].astype(BF16)
    kr = kr_ref[...]
    kr_hi = kr.astype(BF16)
    kr_lo = (kr - kr_hi.astype(F32)).astype(BF16)
    k = _dot(ckv, wk_ref[...]) + (_dot(kr_hi, e_ref[...]) + _dot(kr_lo, e_ref[...]))
    first_m = _first_mla((tm, LANES))
    ks = []
    for hh in range(MLA_HEADS):
        xb = _norm_pad(k[:, hh * MLA_PAD:(hh + 1) * MLA_PAD], g_ref[...], MLA_QK)
        if rope:
            xb = _rope_blk(xb, cm_ref[...], sm_ref[...], MLA_ROPE // 2, first_m)
        ks.append(xb.astype(BF16))
    k_ref[...] = jnp.concatenate(ks, axis=1)
    v_ref[...] = _dot(ckv, wv_ref[...]).astype(BF16)


def _mla_keys(ckv, kr, wk, wv, e_mat, mkg, rope_tabs, seq, tm):
    n = ckv.shape[0]
    rope = rope_tabs is not None
    row = lambda w: pl.BlockSpec((tm, w), lambda i: (i, 0))
    in_specs = [row(MLA_KV_RANK), row(MLA_ROPE),
                _full_spec((MLA_KV_RANK, MLA_HEADS * MLA_PAD)),
                _full_spec((MLA_KV_RANK, MLA_HEADS * MLA_V)),
                _full_spec((MLA_ROPE, MLA_HEADS * MLA_PAD)),
                _full_spec((1, LANES))]
    args = [ckv, kr, wk, wv, e_mat, mkg]
    if rope:
        per = seq // tm
        in_specs += [pl.BlockSpec((tm, LANES), lambda i: (i % per, 0))] * 2
        args += list(rope_tabs)
    return pl.pallas_call(
        functools.partial(_mla_keys_kernel, rope=rope, tm=tm),
        grid=(n // tm,),
        in_specs=in_specs,
        out_specs=[row(MLA_HEADS * MLA_PAD), row(MLA_HEADS * MLA_V)],
        out_shape=[jax.ShapeDtypeStruct((n, MLA_HEADS * MLA_PAD), BF16),
                   jax.ShapeDtypeStruct((n, MLA_HEADS * MLA_V), BF16)],
        compiler_params=_params(("parallel",)),
        name="mla_keys",
    )(*args)


def _axial_tables(n, rot_dim):
    rows = n // GRID_W
    row = jnp.repeat(jnp.arange(rows, dtype=jnp.int32), GRID_W).astype(F32)
    col = jnp.tile(jnp.arange(GRID_W, dtype=jnp.int32), rows).astype(F32)
    nf = rot_dim // 4
    inv = ROPE_THETA ** (-jnp.arange(nf, dtype=F32) / nf)
    ang = jnp.concatenate([row[:, None] * inv, col[:, None] * inv], axis=-1)
    return jnp.cos(ang), jnp.sin(ang)


def _rope_tabs64(n):
    cos, sin = _axial_tables(n, HEAD_DIM)
    return (jnp.concatenate([cos, cos, cos, cos], axis=-1),
            jnp.concatenate([-sin, sin, -sin, sin], axis=-1))


def _rope_tabs_mla(n):
    cos, sin = _axial_tables(n, MLA_ROPE)
    one = jnp.ones((n, MLA_NOPE), F32)
    zero = jnp.zeros((n, MLA_NOPE), F32)
    pad = jnp.zeros((n, MLA_PAD - MLA_QK), F32)
    return (jnp.concatenate([one, cos, cos, pad], axis=-1),
            jnp.concatenate([zero, -sin, sin, pad], axis=-1))


def _tile_gain(g, reps):
    return jnp.tile(g, reps).reshape(1, -1)


def _pad_gain_mla(g):
    return jnp.concatenate([g, jnp.zeros((MLA_PAD - MLA_QK,), F32)]).reshape(1, -1)


def kernel(x_prompt, x_sample, cache_l0_gqa_k, cache_l0_gqa_v, cache_l1_diff_k, cache_l1_diff_v, cache_l1_mla_ckv, cache_l1_mla_krope, c, c_ctx, l0_ada_w, l0_ada_b, l0_norm1_g, l0_norm2_g, l0_w_in, l0_conv_dw_w, l0_conv_dw_b, l0_conv_ln_g, l0_conv_ln_b, l0_q_norm_g, l0_k_norm_g, l0_w_out, l0_ffn_w1, l0_ffn_w3, l0_ffn_w2, l1_ada_w, l1_ada_b, l1_norm1_g, l1_norm2_g, l1_w_in, l1_diff_q_norm_g, l1_diff_k_norm_g, l1_lambda_q1, l1_lambda_k1, l1_lambda_q2, l1_lambda_k2, l1_diff_subln_g, l1_mla_q_a_norm_g, l1_mla_w_qb, l1_mla_kv_a_norm_g, l1_mla_w_kvb, l1_mla_q_norm_g, l1_mla_k_norm_g, l1_w_out, l1_router_w, l1_moe_w1, l1_moe_w3, l1_moe_w2):
    pb, ps, _ = x_prompt.shape
    sb, ss, _ = x_sample.shape
    past = cache_l0_gqa_k.shape[1]
    n_p = pb * ps
    n_s = sb * ss

    cond16 = jnp.concatenate([c, c_ctx[None, :], jnp.zeros((16 - sb - 1, D_MODEL), F32)], axis=0)

    def mod_table(ada_w, ada_b):
        m = _ada(cond16, ada_w, ada_b).reshape(16, 6, D_MODEL)
        return jnp.pad(m, ((0, 0), (0, 2), (0, 0)))

    mod0 = mod_table(l0_ada_w, l0_ada_b)
    mod1 = mod_table(l1_ada_w, l1_ada_b)

    tabs64 = _rope_tabs64(ss)
    tabs_mla = _rope_tabs_mla(ss)

    xp = x_prompt.reshape(n_p, D_MODEL)
    xs = x_sample.reshape(n_s, D_MODEL)

    w_in0 = l0_w_in.astype(BF16)
    qg0 = _tile_gain(l0_q_norm_g, 2)
    kg0 = _tile_gain(l0_k_norm_g, 2)
    n1g0 = l0_norm1_g.reshape(1, -1)
    n2g0 = l0_norm2_g.reshape(1, -1)
    wo0 = l0_w_out.astype(BF16)
    nc0 = D_FF // D_FF_EXPERT
    w1_0 = l0_ffn_w1.astype(BF16).reshape(D_MODEL, nc0, D_FF_EXPERT).transpose(1, 0, 2)
    w3_0 = l0_ffn_w3.astype(BF16).reshape(D_MODEL, nc0, D_FF_EXPERT).transpose(1, 0, 2)
    w2_0 = l0_ffn_w2.astype(BF16).reshape(nc0, D_FF_EXPERT, D_MODEL)

    def layer0(x, seq, bsz, ctx, tabs, cache):
        u, q, k, v = _proj0(x, mod0, n1g0, w_in0, qg0, kg0, tabs, seq, ctx)
        cv = _conv(u.reshape(bsz, seq, 2 * CONV_CH), l0_conv_dw_w, l0_conv_dw_b, l0_conv_ln_g, l0_conv_ln_b,
                   ts=min(seq, 512))
        k3 = k.reshape(bsz, seq, LANES)
        v3 = v.reshape(bsz, seq, LANES)
        segs = [(k3, v3)]
        if cache is not None:
            segs.append(cache)
        at = _gqa_attn(q.reshape(bsz, seq, 512), segs, tq=256)
        y = _post(x, cv.reshape(-1, CONV_CH), at.reshape(-1, 512), mod0, n2g0, wo0[:CONV_CH], wo0[CONV_CH:],
                  w1_0, w3_0, w2_0, seq, ctx)
        return y, k, v

    yp0, k0, v0 = layer0(xp, ps, pb, True, None, None)
    cache0 = (cache_l0_gqa_k.reshape(sb, past, LANES), cache_l0_gqa_v.reshape(sb, past, LANES))
    ys0, _, _ = layer0(xs, ss, sb, False, tabs64, cache0)

    w_in1 = jnp.pad(l1_w_in, ((0, 0), (0, ODD_IN_PAD - l1_w_in.shape[1]))).astype(BF16)
    dqg = _tile_gain(l1_diff_q_norm_g, 2)
    dkg = _tile_gain(l1_diff_k_norm_g, 2)
    qag = l1_mla_q_a_norm_g.reshape(1, -1)
    kvg = l1_mla_kv_a_norm_g.reshape(1, -1)
    wqb = jnp.pad(l1_mla_w_qb.reshape(MLA_Q_RANK, MLA_HEADS, MLA_QK),
                  ((0, 0), (0, 0), (0, MLA_PAD - MLA_QK))).reshape(MLA_Q_RANK, -1).astype(BF16)
    mqg = _pad_gain_mla(l1_mla_q_norm_g)
    mkg = _pad_gain_mla(l1_mla_k_norm_g)
    wkvb = l1_mla_w_kvb.reshape(MLA_KV_RANK, MLA_HEADS, MLA_NOPE + MLA_V)
    wk = jnp.pad(wkvb[:, :, :MLA_NOPE], ((0, 0), (0, 0), (0, MLA_PAD - MLA_NOPE))).reshape(MLA_KV_RANK, -1)
    wk = wk.astype(BF16)
    wv = wkvb[:, :, MLA_NOPE:].reshape(MLA_KV_RANK, -1).astype(BF16)
    e_one = jnp.concatenate([jnp.zeros((MLA_ROPE, MLA_NOPE), F32), jnp.eye(MLA_ROPE, dtype=F32),
                             jnp.zeros((MLA_ROPE, MLA_PAD - MLA_QK), F32)], axis=1)
    e_mat = jnp.tile(e_one, (1, MLA_HEADS)).astype(BF16)
    n1g1 = l1_norm1_g.reshape(1, -1)
    n2g1 = l1_norm2_g.reshape(1, -1)
    wo1 = l1_w_out.astype(BF16)
    rw = jnp.pad(l1_router_w, ((0, 0), (0, LANES - N_EXPERTS)))
    rw_hi = rw.astype(BF16)
    rw_lo = (rw - rw_hi.astype(F32)).astype(BF16)
    mw1 = l1_moe_w1.astype(BF16)
    mw3 = l1_moe_w3.astype(BF16)
    mw2 = l1_moe_w2.astype(BF16)
    lams = (l1_lambda_q1, l1_lambda_k1, l1_lambda_q2, l1_lambda_k2)

    def layer1(x, seq, bsz, ctx, tabs, caches):
        rope_tabs = None if tabs is None else (tabs[0][0], tabs[0][1], tabs[1][0], tabs[1][1])
        dq, dk, dv, qm, ckv, kr = _proj1(x, mod1, n1g1, w_in1, dqg, dkg, qag, wqb, mqg, kvg, rope_tabs, seq, ctx)
        km, vm = _mla_keys(ckv, kr, wk, wv, e_mat, mkg, None if tabs is None else tabs[1], seq, TM)
        dsegs = [(dk.reshape(bsz, seq, DIFF_W), dv.reshape(bsz, seq, DIFF_W))]
        msegs = [(km.reshape(bsz, seq, -1), vm.reshape(bsz, seq, -1))]
        if caches is not None:
            c_dk, c_dv, c_ckv, c_kr = caches
            dsegs.append((c_dk.reshape(sb, past, DIFF_W), c_dv.reshape(sb, past, DIFF_W)))
            ckm, cvm = _mla_keys(c_ckv.reshape(-1, MLA_KV_RANK), c_kr.reshape(-1, MLA_ROPE), wk, wv, e_mat, mkg,
                                 None, past, past)
            msegs.append((ckm.reshape(sb, past, -1), cvm.reshape(sb, past, -1)))
        od = _diff_attn(dq.reshape(bsz, seq, DIFF_W), dsegs, lams, l1_diff_subln_g, tq=256)
        om = _mla_attn(qm.reshape(bsz, seq, -1), msegs, tq=256)
        y = _moe(x, od.reshape(-1, DIFF_W), om.reshape(-1, 512), mod1, n2g1, wo1[:DIFF_W], wo1[DIFF_W:],
                 rw_hi, rw_lo, mw1, mw3, mw2, seq, ctx, tr=256 if ctx else 512)
        return y, dk, dv, ckv, kr

    yp1, dk1, dv1, ckv1, kr1 = layer1(yp0, ps, pb, True, None, None)
    ys1, _, _, _, _ = layer1(ys0, ss, sb, False, (tabs64, tabs_mla),
                             (cache_l1_diff_k, cache_l1_diff_v, cache_l1_mla_ckv, cache_l1_mla_krope))

    return (yp1.reshape(pb, ps, D_MODEL), ys1.reshape(sb, ss, D_MODEL),
            k0.reshape(pb, ps, GQA_KV_HEADS, HEAD_DIM), v0.reshape(pb, ps, GQA_KV_HEADS, HEAD_DIM),
            dk1.reshape(pb, ps, DIFF_HEADS, 2, DIFF_HD), dv1.reshape(pb, ps, DIFF_HEADS, 2 * DIFF_HD),
            ckv1.reshape(pb, ps, MLA_KV_RANK), kr1.reshape(pb, ps, MLA_ROPE))
```

```python
import functools
import math

import jax
import jax.numpy as jnp
from jax import lax
from jax.experimental import pallas as pl
from jax.experimental.pallas import tpu as pltpu

F32 = jnp.float32
BF16 = jnp.bfloat16

D_MODEL = 1024
GRID_W = 64
ROPE_THETA = 10000.0
RMS_EPS = 1e-6
LN_EPS = 1e-5
LOG2E = math.log2(math.e)

CONV_CH = 512
CONV_WIDTH = 31
CONV_PAD = 15
CONV_HALO = 16
HEAD_DIM = 64
GQA_HEADS = 8
GQA_KV_HEADS = 2
GQA_GROUP = 4
EVEN_IN = 1792

DIFF_HD = 64
DIFF_HEADS = 4
DIFF_W = 512
LAMBDA_INIT = 0.8 - 0.6 * math.exp(-0.3 * 1)
MLA_HEADS = 8
MLA_NOPE = 64
MLA_ROPE = 32
MLA_QK = 96
MLA_V = 64
MLA_Q_RANK = 256
MLA_KV_RANK = 128
MLA_PAD = 128
ODD_IN_PAD = 2048

D_FF = 2816
N_EXPERTS = 8
D_FF_EXPERT = 1408

LANES = 128
VMEM_LIMIT = 56 * 1024 * 1024

TM = 512
CONV_RC = 64


def _params(sem):
    return pltpu.CompilerParams(dimension_semantics=sem, vmem_limit_bytes=VMEM_LIMIT)


def _sigmoid(x):
    return 1.0 / (1.0 + jnp.exp(-x))


def _silu(x):
    return x * _sigmoid(x)


def _dot(a, b):
    return jnp.dot(a, b, preferred_element_type=F32)


def _dot_nt(a, b):
    return lax.dot_general(a, b, (((1,), (1,)), ((), ())), preferred_element_type=F32)


def _mod_norm(x, g, sc, sh):
    y = x * lax.rsqrt(jnp.mean(x * x, axis=-1, keepdims=True) + RMS_EPS)
    return (y * g) * (1.0 + sc) + sh


def _lane(shape):
    return lax.broadcasted_iota(jnp.int32, shape, len(shape) - 1)


def _norm64(xb, gain):
    lo = _lane(xb.shape) < 64
    sq = xb * xb
    s_lo = jnp.sum(jnp.where(lo, sq, 0.0), axis=-1, keepdims=True)
    s_hi = jnp.sum(jnp.where(lo, 0.0, sq), axis=-1, keepdims=True)
    ms = jnp.where(lo, s_lo, s_hi) * (1.0 / 64.0)
    return xb * lax.rsqrt(ms + RMS_EPS) * gain


def _norm_pad(xb, gain, n_real):
    ms = jnp.sum(xb * xb, axis=-1, keepdims=True) * (1.0 / n_real)
    return xb * lax.rsqrt(ms + RMS_EPS) * gain


def _rope_blk(xb, cos_t, sin_t, half, first):
    partner = jnp.where(first, pltpu.roll(xb, LANES - half, 1), pltpu.roll(xb, half, 1))
    return xb * cos_t + partner * sin_t


def _first64(shape):
    return (_lane(shape) % 64) < 32


def _first_mla(shape):
    lane = _lane(shape)
    return (lane >= MLA_NOPE) & (lane < MLA_NOPE + MLA_ROPE // 2)


def _ada_kernel(c_ref, w_ref, b_ref, o_ref):
    s = _silu(c_ref[...])
    o_ref[...] = _dot(s.astype(BF16), w_ref[...].astype(BF16)) + b_ref[...]


def _ada(cond16, ada_w, ada_b):
    tn = 1536
    return pl.pallas_call(
        _ada_kernel,
        grid=(6 * D_MODEL // tn,),
        in_specs=[pl.BlockSpec((16, D_MODEL), lambda j: (0, 0)),
                  pl.BlockSpec((D_MODEL, tn), lambda j: (0, j)),
                  pl.BlockSpec((1, tn), lambda j: (0, j))],
        out_specs=pl.BlockSpec((16, tn), lambda j: (0, j)),
        out_shape=jax.ShapeDtypeStruct((16, 6 * D_MODEL), F32),
        compiler_params=_params(("parallel",)),
        name="ada",
    )(cond16, ada_w, ada_b.reshape(1, -1))


def _mod_index(seq, ctx):
    if ctx:
        return lambda i: (8, 0, 0)
    per = seq // TM
    return lambda i: (i // per, 0, 0)


def _row_spec(width):
    return pl.BlockSpec((TM, width), lambda i: (i, 0))


def _full_spec(shape):
    nd = len(shape)
    return pl.BlockSpec(shape, lambda *a: (0,) * nd)


def _rope_spec(seq):
    per = seq // TM
    return pl.BlockSpec((TM, LANES), lambda i: (i % per, 0))


def _proj0_kernel(*refs, rope):
    if rope:
        x_ref, mod_ref, g_ref, w_ref, qg_ref, kg_ref, c_ref, s_ref, u_ref, q_ref, k_ref, v_ref = refs
    else:
        x_ref, mod_ref, g_ref, w_ref, qg_ref, kg_ref, u_ref, q_ref, k_ref, v_ref = refs
    mod = mod_ref[0]
    h = _mod_norm(x_ref[...], g_ref[...], mod[1:2], mod[0:1])
    z = _dot(h.astype(BF16), w_ref[...])
    u_ref[...] = z[:, :2 * CONV_CH]
    base = 2 * CONV_CH
    first = _first64((TM, LANES))
    qs = []
    for b in range(GQA_HEADS * HEAD_DIM // LANES):
        xb = _norm64(z[:, base + b * LANES: base + (b + 1) * LANES], qg_ref[...])
        if rope:
            xb = _rope_blk(xb, c_ref[...], s_ref[...], HEAD_DIM // 2, first)
        qs.append((xb * (HEAD_DIM ** -0.5 * LOG2E)).astype(BF16))
    q_ref[...] = jnp.concatenate(qs, axis=1)
    base += GQA_HEADS * HEAD_DIM
    kb = _norm64(z[:, base: base + LANES], kg_ref[...])
    if rope:
        kb = _rope_blk(kb, c_ref[...], s_ref[...], HEAD_DIM // 2, first)
    k_ref[...] = kb
    v_ref[...] = z[:, base + LANES: base + 2 * LANES]


def _proj0(x, mod, n1_g, w_in, qg, kg, rope_tabs, seq, ctx):
    n = x.shape[0]
    rope = rope_tabs is not None
    in_specs = [_row_spec(D_MODEL),
                pl.BlockSpec((1, 8, D_MODEL), _mod_index(seq, ctx)),
                _full_spec((1, D_MODEL)),
                _full_spec((D_MODEL, EVEN_IN)),
                _full_spec((1, LANES)),
                _full_spec((1, LANES))]
    args = [x, mod, n1_g, w_in, qg, kg]
    if rope:
        in_specs += [_rope_spec(seq), _rope_spec(seq)]
        args += list(rope_tabs)
    return pl.pallas_call(
        functools.partial(_proj0_kernel, rope=rope),
        grid=(n // TM,),
        in_specs=in_specs,
        out_specs=[_row_spec(2 * CONV_CH), _row_spec(512), _row_spec(LANES), _row_spec(LANES)],
        out_shape=[jax.ShapeDtypeStruct((n, 2 * CONV_CH), F32),
                   jax.ShapeDtypeStruct((n, 512), BF16),
                   jax.ShapeDtypeStruct((n, LANES), F32),
                   jax.ShapeDtypeStruct((n, LANES), F32)],
        compiler_params=_params(("parallel",)),
        name="proj0",
    )(*args)


def _conv_kernel(um_ref, up_ref, un_ref, w_ref, b_ref, lg_ref, lb_ref, o_ref, gp_ref, *, ts):
    i = pl.program_id(1)
    n_t = pl.num_programs(1)

    def glu(u):
        return u[:, :CONV_CH] * _sigmoid(u[:, CONV_CH:])

    gp_ref[CONV_HALO:CONV_HALO + ts, :] = glu(um_ref[0])
    gp_ref[0:CONV_HALO, :] = jnp.where(i > 0, glu(up_ref[0]), 0.0)
    gp_ref[CONV_HALO + ts:2 * CONV_HALO + ts, :] = jnp.where(i < n_t - 1, glu(un_ref[0]), 0.0)
    w = w_ref[...]
    off = CONV_HALO - CONV_PAD
    for c in range(ts // CONV_RC):
        acc = jnp.zeros((CONV_RC, CONV_CH), F32)
        for k in range(CONV_WIDTH):
            acc = acc + gp_ref[pl.ds(c * CONV_RC + k + off, CONV_RC), :] * w[k:k + 1, :]
        y = acc + b_ref[...]
        mu = jnp.mean(y, axis=-1, keepdims=True)
        yc = y - mu
        var = jnp.mean(yc * yc, axis=-1, keepdims=True)
        yn = yc * lax.rsqrt(var + LN_EPS) * lg_ref[...] + lb_ref[...]
        o_ref[0, c * CONV_RC:(c + 1) * CONV_RC, :] = _silu(yn).astype(BF16)


def _conv(u, dw_w, dw_b, ln_g, ln_b, ts):
    bsz, seq, _ = u.shape
    n_t = seq // ts
    hb = ts // CONV_HALO
    last_hb = seq // CONV_HALO - 1
    return pl.pallas_call(
        functools.partial(_conv_kernel, ts=ts),
        grid=(bsz, n_t),
        in_specs=[pl.BlockSpec((1, ts, 2 * CONV_CH), lambda b, i: (b, i, 0)),
                  pl.BlockSpec((1, CONV_HALO, 2 * CONV_CH), lambda b, i: (b, jnp.maximum(i * hb - 1, 0), 0)),
                  pl.BlockSpec((1, CONV_HALO, 2 * CONV_CH),
                               lambda b, i: (b, jnp.minimum((i + 1) * hb, last_hb), 0)),
                  _full_spec((CONV_WIDTH, CONV_CH)),
                  _full_spec((1, CONV_CH)),
                  _full_spec((1, CONV_CH)),
                  _full_spec((1, CONV_CH))],
        out_specs=pl.BlockSpec((1, ts, CONV_CH), lambda b, i: (b, i, 0)),
        out_shape=jax.ShapeDtypeStruct((bsz, seq, CONV_CH), BF16),
        scratch_shapes=[pltpu.VMEM((ts + 2 * CONV_HALO, CONV_CH), F32)],
        compiler_params=_params(("parallel", "parallel")),
        name="conv",
    )(u, u, u, dw_w, dw_b.reshape(1, -1), ln_g.reshape(1, -1), ln_b.reshape(1, -1))


def _softmax_pv(q, ks, vs):
    ss = [_dot_nt(q, k) for k in ks]
    m = ss[0].max(axis=-1, keepdims=True)
    for s in ss[1:]:
        m = jnp.maximum(m, s.max(axis=-1, keepdims=True))
    o = None
    for s, v in zip(ss, vs):
        t = _dot(jnp.exp2(s - m).astype(BF16), v)
        o = t if o is None else o + t
    return o


def _half_with_ones(vblock, upper):
    lane = _lane(vblock.shape)
    keep = (lane >= 64) if upper else (lane < 64)
    return jnp.where(keep, vblock, jnp.ones_like(vblock))


def _normalise_half(t, upper):
    if upper:
        return t[:, 64:] * (1.0 / t[:, 0:1])
    return t[:, :64] * (1.0 / t[:, 64:65])


def _gqa_kernel(*refs, n_seg, tq):
    q_ref = refs[0]
    kv = refs[1:1 + 2 * n_seg]
    o_ref = refs[1 + 2 * n_seg]
    q = q_ref[0]
    vfull = [kv[2 * s + 1][0].astype(BF16) for s in range(n_seg)]
    outs = []
    for g in range(GQA_KV_HEADS):
        sl = slice(g * HEAD_DIM, (g + 1) * HEAD_DIM)
        qs = jnp.concatenate(
            [q[:, (GQA_GROUP * g + j) * HEAD_DIM:(GQA_GROUP * g + j + 1) * HEAD_DIM] for j in range(GQA_GROUP)],
            axis=0)
        ks = [kv[2 * s][0, :, sl].astype(BF16) for s in range(n_seg)]
        vs = [_half_with_ones(v, g == 1) for v in vfull]
        o = _normalise_half(_softmax_pv(qs, ks, vs), g == 1)
        for j in range(GQA_GROUP):
            outs.append(o[j * tq:(j + 1) * tq])
    o_ref[0] = jnp.concatenate(outs, axis=1).astype(BF16)


def _kv_specs(segs):
    specs = []
    for k, v in segs:
        specs.append(pl.BlockSpec((1,) + k.shape[1:], lambda b, i: (b, 0, 0)))
        specs.append(pl.BlockSpec((1,) + v.shape[1:], lambda b, i: (b, 0, 0)))
    return specs


def _gqa_attn(q, segs, tq):
    bsz, sq, _ = q.shape
    args = [q]
    for k, v in segs:
        args += [k, v]
    return pl.pallas_call(
        functools.partial(_gqa_kernel, n_seg=len(segs), tq=tq),
        grid=(bsz, sq // tq),
        in_specs=[pl.BlockSpec((1, tq, 512), lambda b, i: (b, i, 0))] + _kv_specs(segs),
        out_specs=pl.BlockSpec((1, tq, 512), lambda b, i: (b, i, 0)),
        out_shape=jax.ShapeDtypeStruct((bsz, sq, 512), BF16),
        compiler_params=_params(("parallel", "parallel")),
        name="gqa_attn",
    )(*args)


def _diff_kernel(*refs, n_seg):
    q_ref = refs[0]
    kv = refs[1:1 + 2 * n_seg]
    lq1, lk1, lq2, lk2, sg_ref, o_ref = refs[1 + 2 * n_seg:]
    lam = (jnp.exp(jnp.sum(lq1[...] * lk1[...], axis=-1, keepdims=True))
           - jnp.exp(jnp.sum(lq2[...] * lk2[...], axis=-1, keepdims=True)) + LAMBDA_INIT)
    q = q_ref[0]
    dv = 2 * DIFF_HD
    outs = []
    for h in range(DIFF_HEADS):
        vs = []
        for s in range(n_seg):
            v = kv[2 * s + 1][0, :, h * dv:(h + 1) * dv].astype(BF16)
            vs.append(jnp.concatenate([v, jnp.ones_like(v)], axis=1))
        o12 = []
        for j in range(2):
            sl = slice((2 * h + j) * DIFF_HD, (2 * h + j + 1) * DIFF_HD)
            ks = [kv[2 * s][0, :, sl].astype(BF16) for s in range(n_seg)]
            t = _softmax_pv(q[:, sl], ks, vs)
            o12.append(t[:, :dv] * (1.0 / t[:, dv:dv + 1]))
        dlt = o12[0] - lam * o12[1]
        ms = jnp.mean(dlt * dlt, axis=-1, keepdims=True)
        od = dlt * lax.rsqrt(ms + RMS_EPS) * sg_ref[...] * (1.0 - LAMBDA_INIT)
        outs.append(od.astype(BF16))
    o_ref[0] = jnp.concatenate(outs, axis=1)


def _diff_attn(q, segs, lams, sub_g, tq):
    bsz, sq, _ = q.shape
    args = [q]
    for k, v in segs:
        args += [k, v]
    args += [l.reshape(1, -1) for l in lams] + [sub_g.reshape(1, -1)]
    return pl.pallas_call(
        functools.partial(_diff_kernel, n_seg=len(segs)),
        grid=(bsz, sq // tq),
        in_specs=([pl.BlockSpec((1, tq, DIFF_W), lambda b, i: (b, i, 0))] + _kv_specs(segs)
                  + [_full_spec((1, DIFF_HD))] * 4 + [_full_spec((1, 2 * DIFF_HD))]),
        out_specs=pl.BlockSpec((1, tq, DIFF_W), lambda b, i: (b, i, 0)),
        out_shape=jax.ShapeDtypeStruct((bsz, sq, DIFF_W), BF16),
        compiler_params=_params(("parallel", "parallel")),
        name="diff_attn",
    )(*args)


def _mla_kernel(*refs, n_seg):
    q_ref = refs[0]
    kv = refs[1:1 + 2 * n_seg]
    o_ref = refs[1 + 2 * n_seg]
    q = q_ref[0]
    outs = []
    for h in range(MLA_HEADS):
        upper = h % 2 == 1
        blk = slice((h // 2) * LANES, (h // 2 + 1) * LANES)
        ks = [kv[2 * s][0, :, h * MLA_PAD:(h + 1) * MLA_PAD] for s in range(n_seg)]
        vs = [_half_with_ones(kv[2 * s + 1][0, :, blk], upper) for s in range(n_seg)]
        outs.append(_normalise_half(_softmax_pv(q[:, h * MLA_PAD:(h + 1) * MLA_PAD], ks, vs), upper))
    o_ref[0] = jnp.concatenate(outs, axis=1).astype(BF16)


def _mla_attn(q, segs, tq):
    bsz, sq, _ = q.shape
    args = [q]
    for k, v in segs:
        args += [k, v]
    return pl.pallas_call(
        functools.partial(_mla_kernel, n_seg=len(segs)),
        grid=(bsz, sq // tq),
        in_specs=[pl.BlockSpec((1, tq, MLA_HEADS * MLA_PAD), lambda b, i: (b, i, 0))] + _kv_specs(segs),
        out_specs=pl.BlockSpec((1, tq, MLA_HEADS * MLA_V), lambda b, i: (b, i, 0)),
        out_shape=jax.ShapeDtypeStruct((bsz, sq, MLA_HEADS * MLA_V), BF16),
        compiler_params=_params(("parallel", "parallel")),
        name="mla_attn",
    )(*args)


def _post_kernel(x_ref, a_ref, b_ref, mod_ref, g_ref, woa_ref, wob_ref,
                 w1_ref, w3_ref, w2_ref, y_ref, x1_s, h2_s, acc_s):
    j = pl.program_id(1)

    @pl.when(j == 0)
    def _():
        mod = mod_ref[0]
        o = _dot(a_ref[...], woa_ref[...]) + _dot(b_ref[...], wob_ref[...])
        x1 = x_ref[...] + mod[2:3] * o
        x1_s[...] = x1
        h2_s[...] = _mod_norm(x1, g_ref[...], mod[4:5], mod[3:4]).astype(BF16)
        acc_s[...] = jnp.zeros_like(acc_s)

    h2 = h2_s[...]
    act = _silu(_dot(h2, w1_ref[0])) * _dot(h2, w3_ref[0])
    acc_s[...] += _dot(act.astype(BF16), w2_ref[0])

    @pl.when(j == pl.num_programs(1) - 1)
    def _():
        y_ref[...] = x1_s[...] + mod_ref[0][5:6] * acc_s[...]


def _post(x, a, b, mod, n2_g, wo_a, wo_b, w1, w3, w2, seq, ctx):
    n = x.shape[0]
    nc, _, f = w1.shape
    mod_idx = _mod_index(seq, ctx)
    return pl.pallas_call(
        _post_kernel,
        grid=(n // TM, nc),
        in_specs=[pl.BlockSpec((TM, D_MODEL), lambda i, j: (i, 0)),
                  pl.BlockSpec((TM, 512), lambda i, j: (i, 0)),
                  pl.BlockSpec((TM, 512), lambda i, j: (i, 0)),
                  pl.BlockSpec((1, 8, D_MODEL), lambda i, j: mod_idx(i)),
                  _full_spec((1, D_MODEL)),
                  _full_spec((512, D_MODEL)),
                  _full_spec((512, D_MODEL)),
                  pl.BlockSpec((1, D_MODEL, f), lambda i, j: (j, 0, 0)),
                  pl.BlockSpec((1, D_MODEL, f), lambda i, j: (j, 0, 0)),
                  pl.BlockSpec((1, f, D_MODEL), lambda i, j: (j, 0, 0))],
        out_specs=pl.BlockSpec((TM, D_MODEL), lambda i, j: (i, 0)),
        out_shape=jax.ShapeDtypeStruct((n, D_MODEL), F32),
        scratch_shapes=[pltpu.VMEM((TM, D_MODEL), F32), pltpu.VMEM((TM, D_MODEL), BF16),
                        pltpu.VMEM((TM, D_MODEL), F32)],
        compiler_params=_params(("parallel", "arbitrary")),
        name="post_ffn",
    )(x, a, b, mod, n2_g, wo_a, wo_b, w1, w3, w2)


R_E1, R_E2, R_RANK1, R_RANK2, R_G1, R_G2 = range(6)


def _route_kernel(x_ref, a_ref, b_ref, mod_ref, g_ref, woa_ref, wob_ref, rhi_ref, rlo_ref,
                  x1_ref, h2_ref, route_ref, cnt_ref, carry_s):
    i = pl.program_id(0)

    @pl.when(i == 0)
    def _():
        carry_s[...] = jnp.zeros_like(carry_s)

    mod = mod_ref[0]
    o = _dot(a_ref[...], woa_ref[...]) + _dot(b_ref[...], wob_ref[...])
    x1 = x_ref[...] + mod[2:3] * o
    x1_ref[...] = x1
    h2 = _mod_norm(x1, g_ref[...], mod[4:5], mod[3:4])
    h2_ref[...] = h2
    hi = h2.astype(BF16)
    lo = (h2 - hi.astype(F32)).astype(BF16)
    logits = _dot(hi, rhi_ref[...]) + (_dot(lo, rhi_ref[...]) + _dot(hi, rlo_ref[...]))
    lane = _lane(logits.shape)
    logits = jnp.where(lane < N_EXPERTS, logits, -1e30)
    m1 = logits.max(axis=-1, keepdims=True)
    i1 = jnp.min(jnp.where(logits == m1, lane, LANES), axis=-1, keepdims=True)
    rest = jnp.where(lane == i1, -jnp.inf, logits)
    m2 = rest.max(axis=-1, keepdims=True)
    i2 = jnp.min(jnp.where(rest == m2, lane, LANES), axis=-1, keepdims=True)
    e = jnp.exp(m2 - m1)
    g1 = 1.0 / (1.0 + e)
    g2 = e * g1
    sel1 = lane == i1
    sel2 = lane == i2
    onehot = jnp.where(sel1, 1.0, 0.0) + jnp.where(sel2, 1.0, 0.0)
    rr = lax.broadcasted_iota(jnp.int32, (TM, TM), 0)
    cc = lax.broadcasted_iota(jnp.int32, (TM, TM), 1)
    tri = jnp.where(cc < rr, 1.0, 0.0).astype(BF16)
    cum = _dot(tri, onehot.astype(BF16)) + carry_s[...]
    rank1 = jnp.sum(jnp.where(sel1, cum, 0.0), axis=-1, keepdims=True)
    rank2 = jnp.sum(jnp.where(sel2, cum, 0.0), axis=-1, keepdims=True)
    rec = jnp.zeros(logits.shape, F32)
    for ln, val in ((R_E1, i1.astype(F32)), (R_E2, i2.astype(F32)), (R_RANK1, rank1), (R_RANK2, rank2),
                    (R_G1, g1), (R_G2, g2)):
        rec = jnp.where(lane == ln, val, rec)
    route_ref[...] = rec
    carry = carry_s[...] + jnp.sum(onehot, axis=0, keepdims=True)
    carry_s[...] = carry
    cnt_ref[...] = jnp.broadcast_to(carry, cnt_ref.shape)


def _route(x, a, b, mod, n2_g, wo_a, wo_b, rw_hi, rw_lo, seq, ctx):
    n = x.shape[0]
    return pl.pallas_call(
        _route_kernel,
        grid=(n // TM,),
        in_specs=[_row_spec(D_MODEL), _row_spec(512), _row_spec(512),
                  pl.BlockSpec((1, 8, D_MODEL), _mod_index(seq, ctx)),
                  _full_spec((1, D_MODEL)),
                  _full_spec((512, D_MODEL)),
                  _full_spec((512, D_MODEL)),
                  _full_spec((D_MODEL, LANES)),
                  _full_spec((D_MODEL, LANES))],
        out_specs=[_row_spec(D_MODEL), _row_spec(D_MODEL), _row_spec(LANES), _full_spec((8, LANES))],
        out_shape=[jax.ShapeDtypeStruct((n, D_MODEL), F32),
                   jax.ShapeDtypeStruct((n, D_MODEL), F32),
                   jax.ShapeDtypeStruct((n, LANES), F32),
                   jax.ShapeDtypeStruct((8, LANES), F32)],
        scratch_shapes=[pltpu.VMEM((1, LANES), F32)],
        compiler_params=_params(("arbitrary",)),
        name="route",
    )(x, a, b, mod, n2_g, wo_a, wo_b, rw_hi, rw_lo)


def _row_copy(src, src_row, dst, dst_row, sem):
    return pltpu.make_async_copy(src.at[pl.ds(src_row, 1)], dst.at[pl.ds(dst_row, 1)], sem)


def _scatter_kernel(pos_ref, h_ref, xs_in_ref, xs_ref, sem, *, ts):
    del xs_in_ref
    base = pl.program_id(0) * ts

    def issue(t, carry):
        for k in range(2):
            _row_copy(h_ref, t, xs_ref, pos_ref[2 * (base + t) + k], sem).start(priority=k)
        return carry

    lax.fori_loop(0, ts, issue, 0, unroll=8)

    def drain(t, carry):
        for k in range(2):
            _row_copy(h_ref, 0, xs_ref, 0, sem).wait()
        return carry

    lax.fori_loop(0, ts, drain, 0, unroll=8)


def _scatter_rows(pos, h2, xs_zero, ts):
    n = h2.shape[0]
    return pl.pallas_call(
        functools.partial(_scatter_kernel, ts=ts),
        grid_spec=pltpu.PrefetchScalarGridSpec(
            num_scalar_prefetch=1,
            grid=(n // ts,),
            in_specs=[pl.BlockSpec((ts, D_MODEL), lambda i, p: (i, 0)), pl.BlockSpec(memory_space=pl.ANY)],
            out_specs=pl.BlockSpec(memory_space=pl.ANY),
            scratch_shapes=[pltpu.SemaphoreType.DMA(())]),
        out_shape=jax.ShapeDtypeStruct(xs_zero.shape, xs_zero.dtype),
        input_output_aliases={2: 0},
        compiler_params=_params(("arbitrary",)),
        name="moe_scatter",
    )(pos, h2, xs_zero)


def _experts_kernel(te_ref, tv_ref, x_ref, w1_ref, w3_ref, w2_ref, o_ref):
    j = pl.program_id(0)

    @pl.when(tv_ref[j] == 1)
    def _():
        x = x_ref[...].astype(BF16)
        act = _silu(_dot(x, w1_ref[0])) * _dot(x, w3_ref[0])
        o_ref[...] = _dot(act.astype(BF16), w2_ref[0])

    @pl.when(tv_ref[j] == 0)
    def _():
        o_ref[...] = jnp.zeros_like(o_ref)


def _experts(tile_expert, tile_valid, xs, w1, w3, w2, tr):
    rows = xs.shape[0]
    f = w1.shape[2]
    return pl.pallas_call(
        _experts_kernel,
        grid_spec=pltpu.PrefetchScalarGridSpec(
            num_scalar_prefetch=2,
            grid=(rows // tr,),
            in_specs=[pl.BlockSpec((tr, D_MODEL), lambda j, te, tv: (j, 0)),
                      pl.BlockSpec((1, D_MODEL, f), lambda j, te, tv: (te[j], 0, 0)),
                      pl.BlockSpec((1, D_MODEL, f), lambda j, te, tv: (te[j], 0, 0)),
                      pl.BlockSpec((1, f, D_MODEL), lambda j, te, tv: (te[j], 0, 0))],
            out_specs=pl.BlockSpec((tr, D_MODEL), lambda j, te, tv: (j, 0))),
        out_shape=jax.ShapeDtypeStruct((rows, D_MODEL), F32),
        compiler_params=_params(("arbitrary",)),
        name="moe_experts",
    )(tile_expert, tile_valid, xs, w1, w3, w2)


def _combine_kernel(pos_ref, os_ref, x1_ref, route_ref, mod_ref, y_ref, buf, sem, *, tc):
    base = pl.program_id(0) * tc

    def issue(t, carry):
        row = base + t
        for k in range(2):
            _row_copy(os_ref, pos_ref[2 * row + k], buf.at[k], t, sem).start(priority=k)
        return carry

    lax.fori_loop(0, tc, issue, 0, unroll=8)

    def drain(t, carry):
        for k in range(2):
            _row_copy(os_ref, 0, buf.at[k], 0, sem).wait()
        return carry

    lax.fori_loop(0, tc, drain, 0, unroll=8)
    rec = route_ref[...]
    g1 = rec[:, R_G1:R_G1 + 1]
    g2 = rec[:, R_G2:R_G2 + 1]
    y_ref[...] = x1_ref[...] + mod_ref[0][5:6] * (g1 * buf[0] + g2 * buf[1])


def _combine(pos, outs, x1, route, mod, seq, ctx, tc):
    n = x1.shape[0]
    per = seq // tc
    mod_idx = (lambda i, p: (8, 0, 0)) if ctx else (lambda i, p: (i // per, 0, 0))
    return pl.pallas_call(
        functools.partial(_combine_kernel, tc=tc),
        grid_spec=pltpu.PrefetchScalarGridSpec(
            num_scalar_prefetch=1,
            grid=(n // tc,),
            in_specs=[pl.BlockSpec(memory_space=pl.ANY),
                      pl.BlockSpec((tc, D_MODEL), lambda i, p: (i, 0)),
                      pl.BlockSpec((tc, LANES), lambda i, p: (i, 0)),
                      pl.BlockSpec((1, 8, D_MODEL), mod_idx)],
            out_specs=pl.BlockSpec((tc, D_MODEL), lambda i, p: (i, 0)),
            scratch_shapes=[pltpu.VMEM((2, tc, D_MODEL), F32), pltpu.SemaphoreType.DMA(())]),
        out_shape=jax.ShapeDtypeStruct((n, D_MODEL), F32),
        compiler_params=_params(("arbitrary",)),
        name="moe_combine",
    )(pos, outs, x1, route, mod)


def _moe(x, a, b, mod, n2_g, wo_a, wo_b, rw_hi, rw_lo, w1, w3, w2, seq, ctx, tr):
    n = x.shape[0]
    x1, h2, route, cnt = _route(x, a, b, mod, n2_g, wo_a, wo_b, rw_hi, rw_lo, seq, ctx)
    counts = cnt[0, :N_EXPERTS].astype(jnp.int32)
    tiles = jnp.right_shift(counts + (tr - 1), int(math.log2(tr)))
    tile_end = jnp.cumsum(tiles)
    start = (tile_end - tiles) * tr
    n_tiles = 2 * n // tr + N_EXPERTS
    e12 = route[:, R_E1:R_E2 + 1].astype(jnp.int32)
    rank = route[:, R_RANK1:R_RANK2 + 1].astype(jnp.int32)
    experts = jnp.arange(N_EXPERTS, dtype=jnp.int32)
    start_of = jnp.sum(jnp.where(e12[:, :, None] == experts, start, 0), axis=-1)
    pos = (start_of + rank).reshape(-1)
    jt = jnp.arange(n_tiles, dtype=jnp.int32)
    tile_valid = (jt < tile_end[-1]).astype(jnp.int32)
    jc = jnp.minimum(jt, tile_end[-1] - 1)
    tile_expert = jnp.sum((jc[:, None] >= tile_end[None, :]).astype(jnp.int32), axis=1)
    xs = _scatter_rows(pos, h2, jnp.zeros((n_tiles * tr, D_MODEL), F32), ts=512)
    outs = _experts(tile_expert, tile_valid, xs, w1, w3, w2, tr)
    return _combine(pos, outs, x1, route, mod, seq, ctx, tc=512)


def _proj1_kernel(*refs, rope):
    if rope:
        (x_ref, mod_ref, g_ref, w_ref, dqg_ref, dkg_ref, qag_ref, wqb_ref, mqg_ref, kvg_ref,
         c_ref, s_ref, cm_ref, sm_ref, dq_ref, dk_ref, dv_ref, qm_ref, ckv_ref, kr_ref) = refs
    else:
        (x_ref, mod_ref, g_ref, w_ref, dqg_ref, dkg_ref, qag_ref, wqb_ref, mqg_ref, kvg_ref,
         dq_ref, dk_ref, dv_ref, qm_ref, ckv_ref, kr_ref) = refs
    mod = mod_ref[0]
    h = _mod_norm(x_ref[...], g_ref[...], mod[1:2], mod[0:1])
    z = _dot(h.astype(BF16), w_ref[...])
    first = _first64((TM, LANES))
    nb = DIFF_W // LANES
    dqs = []
    dks = []
    for b in range(nb):
        xq = _norm64(z[:, b * LANES:(b + 1) * LANES], dqg_ref[...])
        xk = _norm64(z[:, DIFF_W + b * LANES: DIFF_W + (b + 1) * LANES], dkg_ref[...])
        if rope:
            xq = _rope_blk(xq, c_ref[...], s_ref[...], DIFF_HD // 2, first)
            xk = _rope_blk(xk, c_ref[...], s_ref[...], DIFF_HD // 2, first)
        dqs.append((xq * (DIFF_HD ** -0.5 * LOG2E)).astype(BF16))
        dks.append(xk)
    dq_ref[...] = jnp.concatenate(dqs, axis=1)
    dk_ref[...] = jnp.concatenate(dks, axis=1)
    dv_ref[...] = z[:, 2 * DIFF_W:3 * DIFF_W]
    base = 3 * DIFF_W
    qa = z[:, base:base + MLA_Q_RANK]
    qa = qa * lax.rsqrt(jnp.mean(qa * qa, axis=-1, keepdims=True) + RMS_EPS) * qag_ref[...]
    qm = _dot(qa.astype(BF16), wqb_ref[...])
    first_m = _first_mla((TM, LANES))
    qms = []
    for hh in range(MLA_HEADS):
        xb = _norm_pad(qm[:, hh * MLA_PAD:(hh + 1) * MLA_PAD], mqg_ref[...], MLA_QK)
        if rope:
            xb = _rope_blk(xb, cm_ref[...], sm_ref[...], MLA_ROPE // 2, first_m)
        qms.append((xb * (MLA_QK ** -0.5 * LOG2E)).astype(BF16))
    qm_ref[...] = jnp.concatenate(qms, axis=1)
    base += MLA_Q_RANK
    kva = z[:, base:base + MLA_KV_RANK]
    ckv_ref[...] = kva * lax.rsqrt(jnp.mean(kva * kva, axis=-1, keepdims=True) + RMS_EPS) * kvg_ref[...]
    base += MLA_KV_RANK
    kr_ref[...] = z[:, base:base + MLA_ROPE]


def _proj1(x, mod, n1_g, w_in, dqg, dkg, qag, wqb, mqg, kvg, rope_tabs, seq, ctx):
    n = x.shape[0]
    rope = rope_tabs is not None
    in_specs = [_row_spec(D_MODEL),
                pl.BlockSpec((1, 8, D_MODEL), _mod_index(seq, ctx)),
                _full_spec((1, D_MODEL)),
                _full_spec((D_MODEL, ODD_IN_PAD)),
                _full_spec((1, LANES)),
                _full_spec((1, LANES)),
                _full_spec((1, MLA_Q_RANK)),
                _full_spec((MLA_Q_RANK, MLA_HEADS * MLA_PAD)),
                _full_spec((1, LANES)),
                _full_spec((1, MLA_KV_RANK))]
    args = [x, mod, n1_g, w_in, dqg, dkg, qag, wqb, mqg, kvg]
    if rope:
        in_specs += [_rope_spec(seq)] * 4
        args += list(rope_tabs)
    return pl.pallas_call(
        functools.partial(_proj1_kernel, rope=rope),
        grid=(n // TM,),
        in_specs=in_specs,
        out_specs=[_row_spec(DIFF_W), _row_spec(DIFF_W), _row_spec(DIFF_W),
                   _row_spec(MLA_HEADS * MLA_PAD), _row_spec(MLA_KV_RANK), _row_spec(MLA_ROPE)],
        out_shape=[jax.ShapeDtypeStruct((n, DIFF_W), BF16),
                   jax.ShapeDtypeStruct((n, DIFF_W), F32),
                   jax.ShapeDtypeStruct((n, DIFF_W), F32),
                   jax.ShapeDtypeStruct((n, MLA_HEADS * MLA_PAD), BF16),
                   jax.ShapeDtypeStruct((n, MLA_KV_RANK), F32),
                   jax.ShapeDtypeStruct((n, MLA_ROPE), F32)],
        compiler_params=_params(("parallel",)),
        name="proj1",
    )(*args)


def _mla_keys_kernel(*refs, rope, tm):
    if rope:
        ckv_ref, kr_ref, wk_ref, wv_ref, e_ref, g_ref, cm_ref, sm_ref, k_ref, v_ref = refs
    else:
        ckv_ref, kr_ref, wk_ref, wv_ref, e_ref, g_ref, k_ref, v_ref = refs
    ckv = ckv_ref[...].astype(BF16)
    kr = kr_ref[...]
    kr_hi = kr.astype(BF16)
    kr_lo = (kr - kr_hi.astype(F32)).astype(BF16)
    k = _dot(ckv, wk_ref[...]) + (_dot(kr_hi, e_ref[...]) + _dot(kr_lo, e_ref[...]))
    first_m = _first_mla((tm, LANES))
    ks = []
    for hh in range(MLA_HEADS):
        xb = _norm_pad(k[:, hh * MLA_PAD:(hh + 1) * MLA_PAD], g_ref[...], MLA_QK)
        if rope:
            xb = _rope_blk(xb, cm_ref[...], sm_ref[...], MLA_ROPE // 2, first_m)
        ks.append(xb.astype(BF16))
    k_ref[...] = jnp.concatenate(ks, axis=1)
    v_ref[...] = _dot(ckv, wv_ref[...]).astype(BF16)


def _mla_keys(ckv, kr, wk, wv, e_mat, mkg, rope_tabs, seq, tm):
    n = ckv.shape[0]
    rope = rope_tabs is not None
    row = lambda w: pl.BlockSpec((tm, w), lambda i: (i, 0))
    in_specs = [row(MLA_KV_RANK), row(MLA_ROPE),
                _full_spec((MLA_KV_RANK, MLA_HEADS * MLA_PAD)),
                _full_spec((MLA_KV_RANK, MLA_HEADS * MLA_V)),
                _full_spec((MLA_ROPE, MLA_HEADS * MLA_PAD)),
                _full_spec((1, LANES))]
    args = [ckv, kr, wk, wv, e_mat, mkg]
    if rope:
        per = seq // tm
        in_specs += [pl.BlockSpec((tm, LANES), lambda i: (i % per, 0))] * 2
        args += list(rope_tabs)
    return pl.pallas_call(
        functools.partial(_mla_keys_kernel, rope=rope, tm=tm),
        grid=(n // tm,),
        in_specs=in_specs,
        out_specs=[row(MLA_HEADS * MLA_PAD), row(MLA_HEADS * MLA_V)],
        out_shape=[jax.ShapeDtypeStruct((n, MLA_HEADS * MLA_PAD), BF16),
                   jax.ShapeDtypeStruct((n, MLA_HEADS * MLA_V), BF16)],
        compiler_params=_params(("parallel",)),
        name="mla_keys",
    )(*args)


def _axial_tables(n, rot_dim):
    rows = n // GRID_W
    row = jnp.repeat(jnp.arange(rows, dtype=jnp.int32), GRID_W).astype(F32)
    col = jnp.tile(jnp.arange(GRID_W, dtype=jnp.int32), rows).astype(F32)
    nf = rot_dim // 4
    inv = ROPE_THETA ** (-jnp.arange(nf, dtype=F32) / nf)
    ang = jnp.concatenate([row[:, None] * inv, col[:, None] * inv], axis=-1)
    return jnp.cos(ang), jnp.sin(ang)


def _rope_tabs64(n):
    cos, sin = _axial_tables(n, HEAD_DIM)
    return (jnp.concatenate([cos, cos, cos, cos], axis=-1),
            jnp.concatenate([-sin, sin, -sin, sin], axis=-1))


def _rope_tabs_mla(n):
    cos, sin = _axial_tables(n, MLA_ROPE)
    one = jnp.ones((n, MLA_NOPE), F32)
    zero = jnp.zeros((n, MLA_NOPE), F32)
    pad = jnp.zeros((n, MLA_PAD - MLA_QK), F32)
    return (jnp.concatenate([one, cos, cos, pad], axis=-1),
            jnp.concatenate([zero, -sin, sin, pad], axis=-1))


def _tile_gain(g, reps):
    return jnp.tile(g, reps).reshape(1, -1)


def _pad_gain_mla(g):
    return jnp.concatenate([g, jnp.zeros((MLA_PAD - MLA_QK,), F32)]).reshape(1, -1)


def kernel(x_prompt, x_sample, cache_l0_gqa_k, cache_l0_gqa_v, cache_l1_diff_k, cache_l1_diff_v, cache_l1_mla_ckv, cache_l1_mla_krope, c, c_ctx, l0_ada_w, l0_ada_b, l0_norm1_g, l0_norm2_g, l0_w_in, l0_conv_dw_w, l0_conv_dw_b, l0_conv_ln_g, l0_conv_ln_b, l0_q_norm_g, l0_k_norm_g, l0_w_out, l0_ffn_w1, l0_ffn_w3, l0_ffn_w2, l1_ada_w, l1_ada_b, l1_norm1_g, l1_norm2_g, l1_w_in, l1_diff_q_norm_g, l1_diff_k_norm_g, l1_lambda_q1, l1_lambda_k1, l1_lambda_q2, l1_lambda_k2, l1_diff_subln_g, l1_mla_q_a_norm_g, l1_mla_w_qb, l1_mla_kv_a_norm_g, l1_mla_w_kvb, l1_mla_q_norm_g, l1_mla_k_norm_g, l1_w_out, l1_router_w, l1_moe_w1, l1_moe_w3, l1_moe_w2):
    pb, ps, _ = x_prompt.shape
    sb, ss, _ = x_sample.shape
    past = cache_l0_gqa_k.shape[1]
    n_p = pb * ps
    n_s = sb * ss

    cond16 = jnp.concatenate([c, c_ctx[None, :], jnp.zeros((16 - sb - 1, D_MODEL), F32)], axis=0)

    def mod_table(ada_w, ada_b):
        m = _ada(cond16, ada_w, ada_b).reshape(16, 6, D_MODEL)
        return jnp.pad(m, ((0, 0), (0, 2), (0, 0)))

    mod0 = mod_table(l0_ada_w, l0_ada_b)
    mod1 = mod_table(l1_ada_w, l1_ada_b)

    tabs64 = _rope_tabs64(ss)
    tabs_mla = _rope_tabs_mla(ss)

    xp = x_prompt.reshape(n_p, D_MODEL)
    xs = x_sample.reshape(n_s, D_MODEL)

    w_in0 = l0_w_in.astype(BF16)
    qg0 = _tile_gain(l0_q_norm_g, 2)
    kg0 = _tile_gain(l0_k_norm_g, 2)
    n1g0 = l0_norm1_g.reshape(1, -1)
    n2g0 = l0_norm2_g.reshape(1, -1)
    wo0 = l0_w_out.astype(BF16)
    nc0 = D_FF // D_FF_EXPERT
    w1_0 = l0_ffn_w1.astype(BF16).reshape(D_MODEL, nc0, D_FF_EXPERT).transpose(1, 0, 2)
    w3_0 = l0_ffn_w3.astype(BF16).reshape(D_MODEL, nc0, D_FF_EXPERT).transpose(1, 0, 2)
    w2_0 = l0_ffn_w2.astype(BF16).reshape(nc0, D_FF_EXPERT, D_MODEL)

    def layer0(x, seq, bsz, ctx, tabs, cache):
        u, q, k, v = _proj0(x, mod0, n1g0, w_in0, qg0, kg0, tabs, seq, ctx)
        cv = _conv(u.reshape(bsz, seq, 2 * CONV_CH), l0_conv_dw_w, l0_conv_dw_b, l0_conv_ln_g, l0_conv_ln_b,
                   ts=min(seq, 512))
        k3 = k.reshape(bsz, seq, LANES)
        v3 = v.reshape(bsz, seq, LANES)
        segs = [(k3, v3)]
        if cache is not None:
            segs.append(cache)
        at = _gqa_attn(q.reshape(bsz, seq, 512), segs, tq=256)
        y = _post(x, cv.reshape(-1, CONV_CH), at.reshape(-1, 512), mod0, n2g0, wo0[:CONV_CH], wo0[CONV_CH:],
                  w1_0, w3_0, w2_0, seq, ctx)
        return y, k, v

    yp0, k0, v0 = layer0(xp, ps, pb, True, None, None)
    cache0 = (cache_l0_gqa_k.reshape(sb, past, LANES), cache_l0_gqa_v.reshape(sb, past, LANES))
    ys0, _, _ = layer0(xs, ss, sb, False, tabs64, cache0)

    w_in1 = jnp.pad(l1_w_in, ((0, 0), (0, ODD_IN_PAD - l1_w_in.shape[1]))).astype(BF16)
    dqg = _tile_gain(l1_diff_q_norm_g, 2)
    dkg = _tile_gain(l1_diff_k_norm_g, 2)
    qag = l1_mla_q_a_norm_g.reshape(1, -1)
    kvg = l1_mla_kv_a_norm_g.reshape(1, -1)
    wqb = jnp.pad(l1_mla_w_qb.reshape(MLA_Q_RANK, MLA_HEADS, MLA_QK),
                  ((0, 0), (0, 0), (0, MLA_PAD - MLA_QK))).reshape(MLA_Q_RANK, -1).astype(BF16)
    mqg = _pad_gain_mla(l1_mla_q_norm_g)
    mkg = _pad_gain_mla(l1_mla_k_norm_g)
    wkvb = l1_mla_w_kvb.reshape(MLA_KV_RANK, MLA_HEADS, MLA_NOPE + MLA_V)
    wk = jnp.pad(wkvb[:, :, :MLA_NOPE], ((0, 0), (0, 0), (0, MLA_PAD - MLA_NOPE))).reshape(MLA_KV_RANK, -1)
    wk = wk.astype(BF16)
    wv = wkvb[:, :, MLA_NOPE:].reshape(MLA_KV_RANK, -1).astype(BF16)
    e_one = jnp.concatenate([jnp.zeros((MLA_ROPE, MLA_NOPE), F32), jnp.eye(MLA_ROPE, dtype=F32),
                             jnp.zeros((MLA_ROPE, MLA_PAD - MLA_QK), F32)], axis=1)
    e_mat = jnp.tile(e_one, (1, MLA_HEADS)).astype(BF16)
    n1g1 = l1_norm1_g.reshape(1, -1)
    n2g1 = l1_norm2_g.reshape(1, -1)
    wo1 = l1_w_out.astype(BF16)
    rw = jnp.pad(l1_router_w, ((0, 0), (0, LANES - N_EXPERTS)))
    rw_hi = rw.astype(BF16)
    rw_lo = (rw - rw_hi.astype(F32)).astype(BF16)
    mw1 = l1_moe_w1.astype(BF16)
    mw3 = l1_moe_w3.astype(BF16)
    mw2 = l1_moe_w2.astype(BF16)
    lams = (l1_lambda_q1, l1_lambda_k1, l1_lambda_q2, l1_lambda_k2)

    def layer1(x, seq, bsz, ctx, tabs, caches):
        rope_tabs = None if tabs is None else (tabs[0][0], tabs[0][1], tabs[1][0], tabs[1][1])
        dq, dk, dv, qm, ckv, kr = _proj1(x, mod1, n1g1, w_in1, dqg, dkg, qag, wqb, mqg, kvg, rope_tabs, seq, ctx)
        km, vm = _mla_keys(ckv, kr, wk, wv, e_mat, mkg, None if tabs is None else tabs[1], seq, TM)
        dsegs = [(dk.reshape(bsz, seq, DIFF_W), dv.reshape(bsz, seq, DIFF_W))]
        msegs = [(km.reshape(bsz, seq, -1), vm.reshape(bsz, seq, -1))]
        if caches is not None:
            c_dk, c_dv, c_ckv, c_kr = caches
            dsegs.append((c_dk.reshape(sb, past, DIFF_W), c_dv.reshape(sb, past, DIFF_W)))
            ckm, cvm = _mla_keys(c_ckv.reshape(-1, MLA_KV_RANK), c_kr.reshape(-1, MLA_ROPE), wk, wv, e_mat, mkg,
                                 None, past, past)
            msegs.append((ckm.reshape(sb, past, -1), cvm.reshape(sb, past, -1)))
        od = _diff_attn(dq.reshape(bsz, seq, DIFF_W), dsegs, lams, l1_diff_subln_g, tq=256)
        om = _mla_attn(qm.reshape(bsz, seq, -1), msegs, tq=256)
        y = _moe(x, od.reshape(-1, DIFF_W), om.reshape(-1, 512), mod1, n2g1, wo1[:DIFF_W], wo1[DIFF_W:],
                 rw_hi, rw_lo, mw1, mw3, mw2, seq, ctx, tr=256 if ctx else 512)
        return y, dk, dv, ckv, kr

    yp1, dk1, dv1, ckv1, kr1 = layer1(yp0, ps, pb, True, None, None)
    ys1, _, _, _, _ = layer1(ys0, ss, sb, False, (tabs64, tabs_mla),
                             (cache_l1_diff_k, cache_l1_diff_v, cache_l1_mla_ckv, cache_l1_mla_krope))

    return (yp1.reshape(pb, ps, D_MODEL), ys1.reshape(sb, ss, D_MODEL),
            k0.reshape(pb, ps, GQA_KV_HEADS, HEAD_DIM), v0.reshape(pb, ps, GQA_KV_HEADS, HEAD_DIM),
            dk1.reshape(pb, ps, DIFF_HEADS, 2, DIFF_HD), dv1.reshape(pb, ps, DIFF_HEADS, 2 * DIFF_HD),
            ckv1.reshape(pb, ps, MLA_KV_RANK), kr1.reshape(pb, ps, MLA_ROPE))
```
